```python
import jax, jax.numpy as jnp
from jax import lax
import numpy as np

D_MODEL = 2048
BATCH = 1
SEQ = 8192
DEPTH = 2
DEC_BATCH = 8
DEC_SEQ = 4096
PAST_LEN = 128

GDN_HEADS = 8
GDN_DK = 128
GDN_DV = 128
GDN_WIDTH = GDN_HEADS * GDN_DV
GDN_QKV = 2 * GDN_HEADS * GDN_DK + GDN_WIDTH
CONV_K = 5
CHUNK = 64
MLA_HEADS = 8
MLA_NOPE = 128
MLA_ROPE = 64
MLA_DV = 128
MLA_WIDTH = MLA_HEADS * MLA_DV
Q_LORA = 512
KV_LORA = 256
ROPE_BASE = 10000.0
Q_BLOCK = 128
D_MIX = GDN_WIDTH + MLA_WIDTH
EPS = 1e-6
MLA_SCALE = (MLA_NOPE + MLA_ROPE) ** -0.5
GDN_SCALE = GDN_DK ** -0.5

SPLIT_SIZES = (GDN_QKV, GDN_WIDTH, 2 * GDN_HEADS, 2 * GDN_HEADS, Q_LORA, KV_LORA, MLA_ROPE, MLA_WIDTH)
D_IN = sum(SPLIT_SIZES)
SPLIT_IDX = tuple(int(i) for i in np.cumsum(SPLIT_SIZES)[:-1])

kernel_name = "hybrid_gdn_mla_parallel_encoder"


def rmsnorm(x, g):
    xf = x.astype(jnp.float32)
    y = xf * lax.rsqrt(jnp.mean(xf * xf, axis=-1, keepdims=True) + EPS)
    return (y * g.astype(jnp.float32)).astype(x.dtype)


def l2norm(x):
    return x * lax.rsqrt(jnp.sum(x * x, axis=-1, keepdims=True) + EPS)


def centred_depthwise_conv(x, w):
    pad = (CONV_K - 1) // 2
    S = x.shape[1]
    xp = jnp.pad(x, ((0, 0), (pad, pad), (0, 0)))
    out = xp[:, 0:S] * w[0]
    for t in range(1, CONV_K):
        out = out + xp[:, t:t + S] * w[t]
    return out


def gated_delta_rule_chunked(q, k, v, g, beta):
    B, S, H, DK = q.shape
    DV = v.shape[-1]
    N = S // CHUNK

    def chunks(t):
        t = jnp.moveaxis(t, 2, 1)
        return t.reshape((B, H, N, CHUNK) + t.shape[3:])

    q, k, v, g, beta = (chunks(t) for t in (q * GDN_SCALE, k, v, g, beta))
    gc = jnp.cumsum(g, axis=-1)
    causal = jnp.tril(jnp.ones((CHUNK, CHUNK), bool))
    strict = jnp.tril(jnp.ones((CHUNK, CHUNK), bool), -1)
    diff = gc[..., :, None] - gc[..., None, :]
    decay = jnp.where(causal, jnp.exp(jnp.where(causal, diff, 0.0)), 0.0)
    kb = k * beta[..., None]
    L = jnp.where(strict, jnp.einsum('bhnid,bhnjd->bhnij', kb, k) * decay, 0.0)
    A = L + jnp.eye(CHUNK, dtype=L.dtype)
    u = lax.linalg.triangular_solve(A, v * beta[..., None], left_side=True, lower=True, unit_diagonal=True)
    w = lax.linalg.triangular_solve(A, kb * jnp.exp(gc)[..., None], left_side=True, lower=True, unit_diagonal=True)
    attn = jnp.where(causal, jnp.einsum('bhnid,bhnjd->bhnij', q, k) * decay, 0.0)
    q_dec = q * jnp.exp(gc)[..., None]
    g_last = gc[..., -1]
    k_dec = k * jnp.exp(g_last[..., None] - gc)[..., None]

    def step(state, xs):
        q_i, k_i, u_i, w_i, a_i, gl_i = xs
        v_new = u_i - jnp.einsum('bhcd,bhde->bhce', w_i, state)
        o_i = jnp.einsum('bhcd,bhde->bhce', q_i, state) + jnp.einsum('bhcj,bhje->bhce', a_i, v_new)
        state = state * jnp.exp(gl_i)[..., None, None] + jnp.einsum('bhcd,bhce->bhde', k_i, v_new)
        return state, o_i

    xs = tuple(jnp.moveaxis(t, 2, 0) for t in (q_dec, k_dec, u, w, attn, g_last))
    state0 = jnp.zeros((B, H, DK, DV), jnp.float32)
    _, o = lax.scan(step, state0, xs)
    return jnp.transpose(o, (1, 0, 3, 2, 4)).reshape(B, S, H, DV)


def gdn_branch(qkv, b, a, a_log, dt_bias, norm_g):
    B, S, _ = qkv.shape
    qkv = qkv.astype(jnp.float32)
    qd = GDN_HEADS * GDN_DK
    q = l2norm(qkv[..., :qd].reshape(B, S, GDN_HEADS, GDN_DK))
    k = l2norm(qkv[..., qd:2 * qd].reshape(B, S, GDN_HEADS, GDN_DK))
    v = qkv[..., 2 * qd:].reshape(B, S, GDN_HEADS, GDN_DV)
    beta = jax.nn.sigmoid(b.astype(jnp.float32).reshape(B, S, 2, GDN_HEADS))
    g = -jnp.exp(a_log.astype(jnp.float32)) * jax.nn.softplus(
        a.astype(jnp.float32).reshape(B, S, 2, GDN_HEADS) + dt_bias.astype(jnp.float32))
    o_fwd = gated_delta_rule_chunked(q, k, v, g[:, :, 0], beta[:, :, 0])
    flip = lambda t: jnp.flip(t, axis=1)
    o_bwd = flip(gated_delta_rule_chunked(flip(q), flip(k), flip(v), flip(g[:, :, 1]), flip(beta[:, :, 1])))
    o = rmsnorm(o_fwd + o_bwd, norm_g)
    return o.reshape(B, S, GDN_WIDTH)


def rope_tables(S):
    pos = jnp.arange(S, dtype=jnp.float32)
    inv = ROPE_BASE ** (-jnp.arange(0, MLA_ROPE, 2, dtype=jnp.float32) / MLA_ROPE)
    ang = pos[:, None] * inv[None, :]
    return jnp.cos(ang), jnp.sin(ang)


def apply_rope(x, cos, sin):
    xf = x.astype(jnp.float32)
    x1, x2 = xf[..., :MLA_ROPE // 2], xf[..., MLA_ROPE // 2:]
    return jnp.concatenate([x1 * cos - x2 * sin, x1 * sin + x2 * cos], axis=-1).astype(x.dtype)


def dense_attention(q, k, v):
    B, H, S, Dq = q.shape
    Dv = v.shape[-1]
    nb = S // Q_BLOCK
    qb = jnp.transpose(q.reshape(B, H, nb, Q_BLOCK, Dq), (2, 0, 1, 3, 4))

    def one_block(qi):
        s = jnp.einsum('bhqd,bhkd->bhqk', qi, k, preferred_element_type=jnp.float32) * MLA_SCALE
        p = jax.nn.softmax(s, axis=-1)
        return jnp.einsum('bhqk,bhkd->bhqd', p.astype(v.dtype), v)

    o = lax.map(one_block, qb)
    return jnp.transpose(o, (1, 0, 3, 2, 4)).reshape(B, S, H, Dv)


def mla_branch(c_q, c_kv, k_pe, q_norm_g, kv_norm_g, w_uq, w_ukv, cos, sin):
    B, S, _ = c_q.shape
    q = jnp.einsum('bsr,re->bse', rmsnorm(c_q, q_norm_g), w_uq).reshape(B, S, MLA_HEADS, MLA_NOPE + MLA_ROPE)
    kv = jnp.einsum('bsr,re->bse', rmsnorm(c_kv, kv_norm_g), w_ukv).reshape(B, S, MLA_HEADS, MLA_NOPE + MLA_DV)
    q_nope, q_pe = q[..., :MLA_NOPE], q[..., MLA_NOPE:]
    k_nope, v = kv[..., :MLA_NOPE], kv[..., MLA_NOPE:]
    q_pe = apply_rope(q_pe, cos[:, None, :], sin[:, None, :])
    k_pe = apply_rope(k_pe, cos, sin)
    k_pe = jnp.broadcast_to(k_pe[:, :, None, :], (B, S, MLA_HEADS, MLA_ROPE))
    qh = jnp.transpose(jnp.concatenate([q_nope, q_pe], axis=-1), (0, 2, 1, 3))
    kh = jnp.transpose(jnp.concatenate([k_nope, k_pe], axis=-1), (0, 2, 1, 3))
    vh = jnp.transpose(v, (0, 2, 1, 3))
    return dense_attention(qh, kh, vh).reshape(B, S, MLA_WIDTH)


def hybrid_layer(x, pre_g, post_g, w_in, conv_w, a_log, dt_bias, gdn_norm_g,
                 q_norm_g, kv_norm_g, w_uq, w_ukv, w_out, cos, sin):
    h = rmsnorm(x, pre_g)
    proj = jnp.einsum('bsd,de->bse', h, w_in)
    qkv_a, z_a, b_a, a_a, c_q, c_kv, k_pe, z_b = jnp.split(proj, SPLIT_IDX, axis=-1)
    qkv_a = jax.nn.silu(centred_depthwise_conv(qkv_a, conv_w))
    o_a = gdn_branch(qkv_a, b_a, a_a, a_log, dt_bias, gdn_norm_g).astype(x.dtype)
    o_b = mla_branch(c_q, c_kv, k_pe, q_norm_g, kv_norm_g, w_uq, w_ukv, cos, sin)
    mix = jnp.concatenate([o_a * jax.nn.silu(z_a), o_b * jax.nn.silu(z_b)], axis=-1)
    y = jnp.einsum('bse,ed->bsd', mix, w_out)
    return x + rmsnorm(y, post_g)


def run_trunk(x, pre_norm_g, post_norm_g, w_in, conv_w, gdn_a_log, gdn_dt_bias, gdn_norm_g,
              mla_q_norm_g, mla_kv_norm_g, mla_w_uq, mla_w_ukv, w_out):
    cos, sin = rope_tables(x.shape[1])
    for l in range(DEPTH):
        x = hybrid_layer(x, pre_norm_g[l], post_norm_g[l], w_in[l], conv_w[l], gdn_a_log[l],
                         gdn_dt_bias[l], gdn_norm_g[l], mla_q_norm_g[l], mla_kv_norm_g[l],
                         mla_w_uq[l], mla_w_ukv[l], w_out[l], cos, sin)
    return x


def setup_inputs(seed: int = 0) -> dict:
    key = jax.random.key(seed)
    ks = jax.random.split(key, 16)
    f32 = jnp.float32
    nrm = lambda k, shape, scale: jax.random.normal(k, shape, f32) * scale
    gain = lambda k, shape: 1.0 + 0.02 * jax.random.normal(k, shape, f32)
    x_prompt = jax.random.normal(ks[0], (BATCH, SEQ, D_MODEL), f32)
    x_sample = jax.random.normal(ks[1], (DEC_BATCH, DEC_SEQ, D_MODEL), f32)
    pre_norm_g = gain(ks[2], (DEPTH, D_MODEL))
    post_norm_g = gain(ks[3], (DEPTH, D_MODEL))
    w_in = nrm(ks[4], (DEPTH, D_MODEL, D_IN), D_MODEL ** -0.5)
    conv_w = nrm(ks[5], (DEPTH, CONV_K, GDN_QKV), CONV_K ** -0.5)
    gdn_a_log = jnp.log(jax.random.uniform(ks[6], (DEPTH, 2, GDN_HEADS), f32, 1.0, 16.0))
    dt = jnp.exp(jax.random.uniform(ks[7], (DEPTH, 2, GDN_HEADS), f32, np.log(1e-3), np.log(1e-1)))
    gdn_dt_bias = dt + jnp.log(-jnp.expm1(-dt))
    gdn_norm_g = gain(ks[8], (DEPTH, GDN_DV))
    mla_q_norm_g = gain(ks[9], (DEPTH, Q_LORA))
    mla_kv_norm_g = gain(ks[10], (DEPTH, KV_LORA))
    mla_w_uq = nrm(ks[11], (DEPTH, Q_LORA, MLA_HEADS * (MLA_NOPE + MLA_ROPE)), Q_LORA ** -0.5)
    mla_w_ukv = nrm(ks[12], (DEPTH, KV_LORA, MLA_HEADS * (MLA_NOPE + MLA_DV)), KV_LORA ** -0.5)
    w_out = nrm(ks[13], (DEPTH, D_MIX, D_MODEL), D_MIX ** -0.5)
    return {"x_prompt": x_prompt, "x_sample": x_sample, "pre_norm_g": pre_norm_g,
            "post_norm_g": post_norm_g, "w_in": w_in, "conv_w": conv_w, "gdn_a_log": gdn_a_log,
            "gdn_dt_bias": gdn_dt_bias, "gdn_norm_g": gdn_norm_g, "mla_q_norm_g": mla_q_norm_g,
            "mla_kv_norm_g": mla_kv_norm_g, "mla_w_uq": mla_w_uq, "mla_w_ukv": mla_w_ukv,
            "w_out": w_out}


def reference(x_prompt, x_sample, pre_norm_g, post_norm_g, w_in, conv_w, gdn_a_log, gdn_dt_bias,
              gdn_norm_g, mla_q_norm_g, mla_kv_norm_g, mla_w_uq, mla_w_ukv, w_out):
    y_prompt = run_trunk(x_prompt, pre_norm_g, post_norm_g, w_in, conv_w, gdn_a_log, gdn_dt_bias,
                         gdn_norm_g, mla_q_norm_g, mla_kv_norm_g, mla_w_uq, mla_w_ukv, w_out)
    y_sample = run_trunk(x_sample, pre_norm_g, post_norm_g, w_in, conv_w, gdn_a_log, gdn_dt_bias,
                         gdn_norm_g, mla_q_norm_g, mla_kv_norm_g, mla_w_uq, mla_w_ukv, w_out)
    return (y_prompt, y_sample)
```

```python
import functools

import numpy as np
import jax
import jax.numpy as jnp
from jax import lax
from jax.experimental import pallas as pl
from jax.experimental.pallas import tpu as pltpu

F32 = jnp.float32
BF16 = jnp.bfloat16

D_MODEL = 2048
DEPTH = 2
GDN_HEADS = 8
GDN_DK = 128
GDN_DV = 128
GDN_WIDTH = GDN_HEADS * GDN_DV
GDN_QKV = 2 * GDN_HEADS * GDN_DK + GDN_WIDTH
CONV_K = 5
CHUNK = 64
MLA_HEADS = 8
MLA_NOPE = 128
MLA_ROPE = 64
MLA_DV = 128
MLA_WIDTH = MLA_HEADS * MLA_DV
Q_LORA = 512
KV_LORA = 256
ROPE_BASE = 10000.0
D_MIX = GDN_WIDTH + MLA_WIDTH
EPS = 1e-6
MLA_SCALE = (MLA_NOPE + MLA_ROPE) ** -0.5
GDN_SCALE = GDN_DK ** -0.5

LANES = 128
SUBLANES = 8
VMEM_LIMIT = 56 * 1024 * 1024

COL_QKV = 0
COL_ZA = COL_QKV + GDN_QKV
COL_ZB = COL_ZA + GDN_WIDTH
COL_CQ = COL_ZB + MLA_WIDTH
COL_CKV = COL_CQ + Q_LORA
COL_KPE = COL_CKV + KV_LORA
COL_GATE = COL_KPE + 2 * MLA_ROPE
D_PROJ = COL_GATE + LANES
MISC_W = D_PROJ - COL_CQ
QK_PAD = 2 * LANES
TILE = 4 * CHUNK

assert D_PROJ == 6144 and MISC_W == 1024 and COL_GATE - COL_CQ == 896


def _cparams(*sem):
    return pltpu.CompilerParams(dimension_semantics=sem, vmem_limit_bytes=VMEM_LIMIT)


def _silu(z):
    return z / (1.0 + jnp.exp(-z))


def _rms(x, g):
    return x * lax.rsqrt(jnp.mean(x * x, axis=-1, keepdims=True) + EPS) * g


def _inproj_kernel(x_ref, g_ref, w_ref, o_ref, h_ref):
    @pl.when(pl.program_id(1) == 0)
    def _():
        h_ref[...] = _rms(x_ref[...], g_ref[...]).astype(BF16)

    o_ref[...] = jnp.dot(h_ref[...], w_ref[...], preferred_element_type=F32)


def _inproj(x2d, g, w_pad, tm, tn):
    t = x2d.shape[0]
    return pl.pallas_call(
        _inproj_kernel,
        out_shape=jax.ShapeDtypeStruct((t, D_PROJ), F32),
        grid=(t // tm, D_PROJ // tn),
        in_specs=[
            pl.BlockSpec((tm, D_MODEL), lambda i, j: (i, 0)),
            pl.BlockSpec((1, D_MODEL), lambda i, j: (0, 0)),
            pl.BlockSpec((D_MODEL, tn), lambda i, j: (0, j)),
        ],
        out_specs=pl.BlockSpec((tm, tn), lambda i, j: (i, j)),
        scratch_shapes=[pltpu.VMEM((tm, D_MODEL), BF16)],
        compiler_params=_cparams("parallel", "arbitrary"),
        name="inproj",
    )(x2d, g, w_pad)


def _mla_proj_kernel(p_ref, gq_ref, gkv_ref, wq_ref, wkv_ref, cos_ref, sin_ref,
                     q_ref, k_ref, v_ref):
    t = p_ref[...]
    cqn = _rms(t[:, 0:Q_LORA], gq_ref[...]).astype(BF16)
    ckn = _rms(t[:, Q_LORA:Q_LORA + KV_LORA], gkv_ref[...]).astype(BF16)
    kpe2 = t[:, COL_KPE - COL_CQ:COL_GATE - COL_CQ]
    qe = jnp.dot(cqn, wq_ref[...], preferred_element_type=F32)
    kve = jnp.dot(ckn, wkv_ref[...], preferred_element_type=F32)
    c2 = cos_ref[...]
    s2 = sin_ref[...]
    krope = (kpe2 * c2 + pltpu.roll(kpe2, MLA_ROPE, axis=1) * s2).astype(BF16)
    for h in range(MLA_HEADS):
        qp = qe[:, h * QK_PAD + LANES:(h + 1) * QK_PAD]
        qr = qp * c2 + pltpu.roll(qp, MLA_ROPE, axis=1) * s2
        q_ref[0, h, :, 0:LANES] = qe[:, h * QK_PAD:h * QK_PAD + LANES].astype(BF16)
        q_ref[0, h, :, LANES:QK_PAD] = qr.astype(BF16)
        k_ref[0, h, :, 0:LANES] = kve[:, h * QK_PAD:h * QK_PAD + LANES].astype(BF16)
        k_ref[0, h, :, LANES:QK_PAD] = krope
        v_ref[0, h] = kve[:, h * QK_PAD + LANES:(h + 1) * QK_PAD].astype(BF16)


def _mla_proj(proj, gq, gkv, wq_ext, wkv, cos2, sin2, b, s, ts):
    nt = s // ts
    hs = MLA_HEADS
    return pl.pallas_call(
        _mla_proj_kernel,
        out_shape=(jax.ShapeDtypeStruct((b, hs, s, QK_PAD), BF16),
                   jax.ShapeDtypeStruct((b, hs, s, QK_PAD), BF16),
                   jax.ShapeDtypeStruct((b, hs, s, MLA_DV), BF16)),
        grid=(b, nt),
        in_specs=[
            pl.BlockSpec((ts, MISC_W), lambda bi, i: (bi * nt + i, COL_CQ // MISC_W)),
            pl.BlockSpec((1, Q_LORA), lambda bi, i: (0, 0)),
            pl.BlockSpec((1, KV_LORA), lambda bi, i: (0, 0)),
            pl.BlockSpec((Q_LORA, hs * QK_PAD), lambda bi, i: (0, 0)),
            pl.BlockSpec((KV_LORA, hs * QK_PAD), lambda bi, i: (0, 0)),
            pl.BlockSpec((ts, LANES), lambda bi, i: (i, 0)),
            pl.BlockSpec((ts, LANES), lambda bi, i: (i, 0)),
        ],
        out_specs=(pl.BlockSpec((1, hs, ts, QK_PAD), lambda bi, i: (bi, 0, i, 0)),
                   pl.BlockSpec((1, hs, ts, QK_PAD), lambda bi, i: (bi, 0, i, 0)),
                   pl.BlockSpec((1, hs, ts, MLA_DV), lambda bi, i: (bi, 0, i, 0))),
        compiler_params=_cparams("parallel", "parallel"),
        name="mla_proj",
    )(proj, gq, gkv, wq_ext, wkv, cos2, sin2)


def _attn_kernel(q_ref, k_ref, v_ref, z_ref, o_ref, *, tk, nk):
    q = q_ref[0, 0]
    tq = q.shape[0]

    def body(j, carry):
        m, l, acc = carry
        off = pl.multiple_of(j * tk, tk)
        kt = k_ref[0, 0, pl.ds(off, tk), :]
        vt = v_ref[0, 0, pl.ds(off, tk), :]
        s = lax.dot_general(q, kt, (((1,), (1,)), ((), ())),
                            preferred_element_type=F32) * MLA_SCALE
        m_new = jnp.maximum(m, jnp.max(s, axis=-1, keepdims=True))
        p = jnp.exp(s - m_new)
        alpha = jnp.exp(m - m_new)
        l = alpha * l + jnp.sum(p, axis=-1, keepdims=True)
        acc = alpha * acc + jnp.dot(p.astype(BF16), vt, preferred_element_type=F32)
        return m_new, l, acc

    m0 = jnp.full((tq, 1), -jnp.inf, F32)
    l0 = jnp.zeros((tq, 1), F32)
    a0 = jnp.zeros((tq, MLA_DV), F32)
    _, l, acc = lax.fori_loop(0, nk, body, (m0, l0, a0))
    o_ref[0] = ((acc / l) * _silu(z_ref[...])).astype(BF16)


def _attention(q, k, v, proj, b, s, tq, tk):
    hs = MLA_HEADS
    nq = s // tq
    zb0 = COL_ZB // LANES
    return pl.pallas_call(
        functools.partial(_attn_kernel, tk=tk, nk=s // tk),
        out_shape=jax.ShapeDtypeStruct((b, s, MLA_WIDTH), BF16),
        grid=(b, hs, nq),
        in_specs=[
            pl.BlockSpec((1, 1, tq, QK_PAD), lambda bi, h, i: (bi, h, i, 0)),
            pl.BlockSpec((1, 1, s, QK_PAD), lambda bi, h, i: (bi, h, 0, 0)),
            pl.BlockSpec((1, 1, s, MLA_DV), lambda bi, h, i: (bi, h, 0, 0)),
            pl.BlockSpec((tq, LANES), lambda bi, h, i: (bi * nq + i, zb0 + h)),
        ],
        out_specs=pl.BlockSpec((1, tq, MLA_DV), lambda bi, h, i: (bi, i, h)),
        compiler_params=_cparams("parallel", "parallel", "arbitrary"),
        name="attention",
    )(q, k, v, proj)


def _chunk_masks():
    r = lax.broadcasted_iota(jnp.int32, (TILE, TILE), 0)
    c = lax.broadcasted_iota(jnp.int32, (TILE, TILE), 1)
    same = (r // CHUNK) == (c // CHUNK)
    return same & (r >= c), same & (r > c), same & (r <= c), same & (r < c)


def _neumann_inverse(l_mat):
    n = l_mat.shape[0]
    r = lax.broadcasted_iota(jnp.int32, (n, n), 0)
    c = lax.broadcasted_iota(jnp.int32, (n, n), 1)
    eye = (r == c).astype(F32)
    t = eye - l_mat
    p = l_mat
    steps = int(np.log2(CHUNK)) - 1
    for it in range(steps):
        pb = p.astype(BF16)
        p = jnp.dot(pb, pb, preferred_element_type=F32)
        t = t + jnp.dot(t.astype(BF16), p.astype(BF16), preferred_element_type=F32)
    return t


def _col(x, lane):
    li = lax.broadcasted_iota(jnp.int32, x.shape, 1)
    return jnp.sum(jnp.where(li == lane, x, 0.0), axis=1, keepdims=True)


def _gdn_prep_kernel(q_ref, qp_ref, qn_ref, k_ref, kp_ref, kn_ref, v_ref, vp_ref, vn_ref,
                     cwq_ref, cwk_ref, cwv_ref, gate_ref, gpar_ref,
                     uf_ref, wf_ref, qdf_ref, kdf_ref, atf_ref,
                     ub_ref, wb_ref, qdb_ref, kdb_ref, atb_ref, egl_ref,
                     xs_ref, g_ref, csl_ref, csu_ref, cslt_ref, csut_ref):
    i = pl.program_id(1)
    nt = pl.num_programs(1)
    h = pl.program_id(2)
    lower, slower, upper, supper = _chunk_masks()

    @pl.when(h == 0)
    def _():
        t = gate_ref[...]
        a_log = gpar_ref[0:1, :]
        dt_b = gpar_ref[1:2, :]
        li = lax.broadcasted_iota(jnp.int32, t.shape, 1)
        xg = t + dt_b
        sp = jnp.maximum(xg, 0.0) + jnp.log1p(jnp.exp(-jnp.abs(xg)))
        gdec = -jnp.exp(a_log) * sp
        beta = 1.0 / (1.0 + jnp.exp(-t))
        n_gate = 2 * GDN_HEADS
        g = jnp.where(li < n_gate, beta, jnp.where(li < 2 * n_gate, gdec, 0.0))
        g_ref[...] = g
        tri_l = lower.astype(F32)
        tri_u = upper.astype(F32)
        csl = jnp.dot(tri_l, g, preferred_element_type=F32, precision=lax.Precision.HIGHEST)
        csu = jnp.dot(tri_u, g, preferred_element_type=F32, precision=lax.Precision.HIGHEST)
        csl_ref[...] = csl
        csu_ref[...] = csu
        cslt_ref[...] = csl.T
        csut_ref[...] = csu.T

    def conv_silu(main_ref, prev_ref, next_ref, w_ref):
        xs_ref[0:SUBLANES, :] = jnp.where(i > 0, prev_ref[...], 0.0)
        xs_ref[SUBLANES:SUBLANES + TILE, :] = main_ref[...]
        xs_ref[SUBLANES + TILE:, :] = jnp.where(i < nt - 1, next_ref[...], 0.0)
        pad = (CONV_K - 1) // 2
        acc = xs_ref[pl.ds(SUBLANES - pad, TILE), :] * w_ref[0:1, :]
        for tap in range(1, CONV_K):
            acc = acc + xs_ref[pl.ds(SUBLANES - pad + tap, TILE), :] * w_ref[tap:tap + 1, :]
        return _silu(acc)

    def l2n(x):
        return x * lax.rsqrt(jnp.sum(x * x, axis=-1, keepdims=True) + EPS)

    q = l2n(conv_silu(q_ref, qp_ref, qn_ref, cwq_ref)) * GDN_SCALE
    k = l2n(conv_silu(k_ref, kp_ref, kn_ref, cwk_ref))
    v = conv_silu(v_ref, vp_ref, vn_ref, cwv_ref)
    qb = q.astype(BF16)
    kb16 = k.astype(BF16)
    nt_dims = (((1,), (1,)), ((), ()))
    kk = lax.dot_general(kb16, kb16, nt_dims, preferred_element_type=F32)
    qk = lax.dot_general(qb, kb16, nt_dims, preferred_element_type=F32)

    g = g_ref[...]
    csl = csl_ref[...]
    csu = csu_ref[...]
    nh = GDN_HEADS

    def one_direction(beta_lane, g_lane, cs, cs_other, cst_ref, tri, stri,
                      u_ref, w_ref, qd_ref, kd_ref, at_ref, egl_row):
        beta = _col(g, beta_lane)
        gc = _col(cs, g_lane)
        rest = _col(cs_other - g, g_lane)
        gc_row = cst_ref[pl.ds(g_lane, 1), :]
        decay = jnp.where(tri, jnp.exp(jnp.where(tri, gc - gc_row, 0.0)), 0.0)
        l_mat = jnp.where(stri, beta * kk * decay, 0.0)
        t_inv = _neumann_inverse(l_mat).astype(BF16)
        egc = jnp.exp(gc)
        rhs = jnp.concatenate([(v * beta).astype(BF16), (k * (beta * egc)).astype(BF16)], axis=1)
        uw = jnp.dot(t_inv, rhs, preferred_element_type=F32)
        u_ref[0] = uw[:, 0:GDN_DV].astype(BF16)
        w_ref[0] = uw[:, GDN_DV:].astype(BF16)
        qd_ref[0] = (q * egc).astype(BF16)
        kd_ref[0] = (k * jnp.exp(rest)).astype(BF16)
        attn = jnp.where(tri, qk * decay, 0.0)
        half = TILE // 2
        at_ref[0, 0:half, :] = attn[0:half, 0:half].astype(BF16)
        at_ref[0, half:, :] = attn[half:, half:].astype(BF16)
        tot = jnp.broadcast_to(jnp.exp(gc + rest), (TILE, LANES))
        for c in range(TILE // CHUNK):
            egl_ref[0, c, pl.ds(egl_row, 1), :] = tot[c * CHUNK:c * CHUNK + 1, :]

    one_direction(h, 2 * nh + h, csl, csu, cslt_ref, lower, slower,
                  uf_ref, wf_ref, qdf_ref, kdf_ref, atf_ref, h)
    one_direction(nh + h, 3 * nh + h, csu, csl, csut_ref, upper, supper,
                  ub_ref, wb_ref, qdb_ref, kdb_ref, atb_ref, nh + h)


def _gdn_prep(proj, conv_w, gpar, b, s):
    nt = s // TILE
    hs = GDN_HEADS
    rows8 = TILE // SUBLANES
    last8 = b * s // SUBLANES - 1

    def main(col0):
        return pl.BlockSpec((TILE, LANES), lambda bi, i, h: (bi * nt + i, col0 + h))

    def prev(col0):
        return pl.BlockSpec((SUBLANES, LANES),
                            lambda bi, i, h: (jnp.maximum((bi * nt + i) * rows8 - 1, 0), col0 + h))

    def nxt(col0):
        return pl.BlockSpec((SUBLANES, LANES),
                            lambda bi, i, h: (jnp.minimum((bi * nt + i + 1) * rows8, last8), col0 + h))

    def cw(col0):
        return pl.BlockSpec((CONV_K, LANES), lambda bi, i, h: (0, col0 + h))

    qc, kc, vc = 0, hs, 2 * hs
    seq = jax.ShapeDtypeStruct((b, s, GDN_WIDTH), BF16)
    seq_spec = pl.BlockSpec((1, TILE, LANES), lambda bi, i, h: (bi, i, h))
    nchunk = TILE // CHUNK
    egl = jax.ShapeDtypeStruct((b * nt, nchunk, 2 * hs, LANES), F32)
    egl_spec = pl.BlockSpec((1, nchunk, 2 * hs, LANES), lambda bi, i, h: (bi * nt + i, 0, 0, 0))
    return pl.pallas_call(
        _gdn_prep_kernel,
        out_shape=(seq,) * 10 + (egl,),
        grid=(b, nt, hs),
        in_specs=[main(qc), prev(qc), nxt(qc), main(kc), prev(kc), nxt(kc),
                  main(vc), prev(vc), nxt(vc), cw(qc), cw(kc), cw(vc),
                  pl.BlockSpec((TILE, LANES), lambda bi, i, h: (bi * nt + i, COL_GATE // LANES)),
                  pl.BlockSpec((SUBLANES, LANES), lambda bi, i, h: (0, 0))],
        out_specs=(seq_spec,) * 10 + (egl_spec,),
        scratch_shapes=[pltpu.VMEM((TILE + 2 * SUBLANES, LANES), F32),
                        pltpu.VMEM((TILE, LANES), F32), pltpu.VMEM((TILE, LANES), F32),
                        pltpu.VMEM((TILE, LANES), F32), pltpu.VMEM((LANES, TILE), F32),
                        pltpu.VMEM((LANES, TILE), F32)],
        compiler_params=_cparams("parallel", "parallel", "arbitrary"),
        name="gdn_prep",
    )(*([proj] * 9), conv_w, conv_w, conv_w, proj, gpar)


def _gdn_scan_kernel(uf_ref, wf_ref, qdf_ref, kdf_ref, atf_ref, eglf_ref,
                     ub_ref, wb_ref, qdb_ref, kdb_ref, atb_ref, eglb_ref,
                     of_ref, ob_ref, st_ref):
    @pl.when(pl.program_id(1) == 0)
    def _():
        st_ref[...] = jnp.zeros_like(st_ref)

    nchunk = TILE // CHUNK
    tn_dims = (((0,), (0,)), ((), ()))

    def chain(idx, c, hd, u_ref, w_ref, qd_ref, kd_ref, at_ref, egl_ref, o_ref, egl_row):
        rows = pl.ds(pl.multiple_of(c * CHUNK, CHUNK), CHUNK)
        cols = slice(hd * LANES, (hd + 1) * LANES)
        state = st_ref[idx]
        sb = state.astype(BF16)
        ws = jnp.dot(w_ref[0, rows, cols], sb, preferred_element_type=F32)
        qs = jnp.dot(qd_ref[0, rows, cols], sb, preferred_element_type=F32)
        v_new = (u_ref[0, rows, cols].astype(F32) - ws).astype(BF16)
        v_pair = jnp.concatenate([v_new, v_new], axis=0)
        o_ref[0, rows, cols] = qs + jnp.dot(at_ref[0, rows, cols], v_pair,
                                            preferred_element_type=F32)
        decay = egl_ref[0, pl.ds(c, 1), egl_row, :]
        upd = lax.dot_general(kd_ref[0, rows, cols], v_new, tn_dims, preferred_element_type=F32)
        st_ref[idx] = state * decay + upd

    def step(c, carry):
        for hd in range(GDN_HEADS):
            chain(hd, c, hd, uf_ref, wf_ref, qdf_ref, kdf_ref, atf_ref, eglf_ref, of_ref, hd)
            chain(GDN_HEADS + hd, nchunk - 1 - c, hd, ub_ref, wb_ref, qdb_ref, kdb_ref, atb_ref,
                  eglb_ref, ob_ref, GDN_HEADS + hd)
        return carry

    lax.fori_loop(0, nchunk, step, 0)


def _gdn_scan(prep, b, s):
    uf, wf, qdf, kdf, atf, ub, wb, qdb, kdb, atb, egl = prep
    nt = s // TILE
    hs = GDN_HEADS
    nchunk = TILE // CHUNK
    fwd = pl.BlockSpec((1, TILE, GDN_WIDTH), lambda bi, i: (bi, i, 0))
    bwd = pl.BlockSpec((1, TILE, GDN_WIDTH), lambda bi, i: (bi, nt - 1 - i, 0))
    egl_f = pl.BlockSpec((1, nchunk, 2 * hs, LANES), lambda bi, i: (bi * nt + i, 0, 0, 0))
    egl_b = pl.BlockSpec((1, nchunk, 2 * hs, LANES), lambda bi, i: (bi * nt + nt - 1 - i, 0, 0, 0))
    out = jax.ShapeDtypeStruct((b, s, GDN_WIDTH), F32)
    return pl.pallas_call(
        _gdn_scan_kernel,
        out_shape=(out, out),
        grid=(b, nt),
        in_specs=[fwd] * 5 + [egl_f] + [bwd] * 5 + [egl_b],
        out_specs=(fwd, bwd),
        scratch_shapes=[pltpu.VMEM((2 * hs, GDN_DK, GDN_DV), F32)],
        compiler_params=_cparams("parallel", "arbitrary"),
        name="gdn_scan",
    )(uf, wf, qdf, kdf, atf, egl, ub, wb, qdb, kdb, atb, egl)


def _outproj_kernel(of_ref, ob_ref, za_ref, mixb_ref, x_ref, gn_ref, gp_ref, w_ref, y_ref):
    o = of_ref[...] + ob_ref[...]
    za = za_ref[...]
    parts = []
    for h in range(GDN_HEADS):
        sl = slice(h * GDN_DV, (h + 1) * GDN_DV)
        parts.append((_rms(o[:, sl], gn_ref[...]) * _silu(za[:, sl])).astype(BF16))
    mix_a = jnp.concatenate(parts, axis=1)
    y = jnp.dot(mix_a, w_ref[0:GDN_WIDTH, :], preferred_element_type=F32)
    y = y + jnp.dot(mixb_ref[...], w_ref[GDN_WIDTH:, :], preferred_element_type=F32)
    y_ref[...] = x_ref[...] + _rms(y, gp_ref[...])


def _outproj(o_f, o_b, proj, mix_b, x2d, gn, gp, w_out, tm):
    t = x2d.shape[0]
    row = lambda i: (i, 0)
    const = lambda i: (0, 0)
    return pl.pallas_call(
        _outproj_kernel,
        out_shape=jax.ShapeDtypeStruct((t, D_MODEL), F32),
        grid=(t // tm,),
        in_specs=[
            pl.BlockSpec((tm, GDN_WIDTH), row),
            pl.BlockSpec((tm, GDN_WIDTH), row),
            pl.BlockSpec((tm, GDN_WIDTH), lambda i: (i, COL_ZA // GDN_WIDTH)),
            pl.BlockSpec((tm, MLA_WIDTH), row),
            pl.BlockSpec((tm, D_MODEL), row),
            pl.BlockSpec((1, GDN_DV), const),
            pl.BlockSpec((1, D_MODEL), const),
            pl.BlockSpec((D_MIX, D_MODEL), const),
        ],
        out_specs=pl.BlockSpec((tm, D_MODEL), row),
        compiler_params=_cparams("parallel"),
        name="outproj",
    )(o_f, o_b, proj, mix_b, x2d, gn, gp, w_out)


def _swap_halves(w):
    half = MLA_ROPE // 2
    return jnp.concatenate([w[..., half:], w[..., :half]], axis=-1)


def _prep_layer(w_in, mla_w_uq, mla_w_ukv, w_out, a_log, dt_bias):
    o = np.cumsum((0, GDN_QKV, GDN_WIDTH, 2 * GDN_HEADS, 2 * GDN_HEADS, Q_LORA, KV_LORA,
                   MLA_ROPE, MLA_WIDTH))
    qkv, za, bl, al, cq, ckv, kpe, zb = (w_in[:, o[n]:o[n + 1]] for n in range(8))
    pad = jnp.zeros((D_MODEL, LANES - 4 * GDN_HEADS), w_in.dtype)
    w_pad = jnp.concatenate([qkv, za, zb, cq, ckv, kpe, _swap_halves(kpe), bl, al, pad],
                            axis=1).astype(BF16)
    wq = mla_w_uq.reshape(Q_LORA, MLA_HEADS, MLA_NOPE + MLA_ROPE)
    wq_ext = jnp.concatenate([wq, _swap_halves(wq[..., MLA_NOPE:])], axis=-1)
    wq_ext = wq_ext.reshape(Q_LORA, MLA_HEADS * QK_PAD).astype(BF16)
    wkv = mla_w_ukv.astype(BF16)
    gpar = jnp.zeros((SUBLANES, LANES), F32)
    n_gate = 2 * GDN_HEADS
    gpar = gpar.at[0, n_gate:2 * n_gate].set(a_log.reshape(-1))
    gpar = gpar.at[1, n_gate:2 * n_gate].set(dt_bias.reshape(-1))
    return w_pad, wq_ext, wkv, w_out.astype(BF16), gpar


def _rope_tables(s):
    pos = jnp.arange(s, dtype=F32)
    inv = ROPE_BASE ** (-jnp.arange(0, MLA_ROPE, 2, dtype=F32) / MLA_ROPE)
    ang = pos[:, None] * inv[None, :]
    cos, sin = jnp.cos(ang), jnp.sin(ang)
    zero = jnp.zeros((s, LANES - MLA_ROPE), F32)
    return (jnp.concatenate([cos, cos, zero], axis=1),
            jnp.concatenate([-sin, sin, zero], axis=1))


def _pick(n, prefs):
    for p in prefs:
        if n % p == 0:
            return p
    return n


def _tiles(b, s):
    t = b * s
    return dict(tm_in=_pick(t, (1024, 512, 256)), tn_in=1024,
                ts_mla=_pick(s, (512, 256)),
                tq=_pick(s, (512, 256, 128)), tk=_pick(s, (1024, 512, 256, 128)),
                tm_out=_pick(t, (512, 256)))


def _layer(x2d, b, s, tl, pre_g, post_g, conv_w, gdn_norm_g, q_norm_g, kv_norm_g,
           w_pad, wq_ext, wkv, w_out, gpar, cos2, sin2):
    proj = _inproj(x2d, pre_g.reshape(1, -1), w_pad, tl["tm_in"], tl["tn_in"])
    prep = _gdn_prep(proj, conv_w, gpar, b, s)
    o_f, o_b = _gdn_scan(prep, b, s)
    q, k, v = _mla_proj(proj, q_norm_g.reshape(1, -1), kv_norm_g.reshape(1, -1), wq_ext, wkv,
                        cos2, sin2, b, s, tl["ts_mla"])
    mix_b = _attention(q, k, v, proj, b, s, tl["tq"], tl["tk"])
    return _outproj(o_f.reshape(b * s, -1), o_b.reshape(b * s, -1), proj,
                    mix_b.reshape(b * s, -1), x2d, gdn_norm_g.reshape(1, -1),
                    post_g.reshape(1, -1), w_out, tl["tm_out"])


def _trunk(x, layers, pre_norm_g, post_norm_g, conv_w, gdn_norm_g, mla_q_norm_g, mla_kv_norm_g):
    b, s, d = x.shape
    assert d == D_MODEL and s % TILE == 0
    tl = _tiles(b, s)
    cos2, sin2 = _rope_tables(s)
    x2d = x.reshape(b * s, d)
    for l in range(DEPTH):
        x2d = _layer(x2d, b, s, tl, pre_norm_g[l], post_norm_g[l], conv_w[l], gdn_norm_g[l],
                     mla_q_norm_g[l], mla_kv_norm_g[l], *layers[l], cos2, sin2)
    return x2d.reshape(b, s, d)


def kernel(x_prompt, x_sample, pre_norm_g, post_norm_g, w_in, conv_w, gdn_a_log, gdn_dt_bias,
           gdn_norm_g, mla_q_norm_g, mla_kv_norm_g, mla_w_uq, mla_w_ukv, w_out):
    layers = [_prep_layer(w_in[l], mla_w_uq[l], mla_w_ukv[l], w_out[l], gdn_a_log[l],
                          gdn_dt_bias[l]) for l in range(DEPTH)]
    args = (layers, pre_norm_g, post_norm_g, conv_w, gdn_norm_g, mla_q_norm_g, mla_kv_norm_g)
    return (_trunk(x_prompt, *args), _trunk(x_sample, *args))
```

```python
import functools

import numpy as np
import jax
import jax.numpy as jnp
from jax import lax
from jax.experimental import pallas as pl
from jax.experimental.pallas import tpu as pltpu

F32 = jnp.float32
BF16 = jnp.bfloat16

D_MODEL = 2048
DEPTH = 2
GDN_HEADS = 8
GDN_DK = 128
GDN_DV = 128
GDN_WIDTH = GDN_HEADS * GDN_DV
GDN_QKV = 2 * GDN_HEADS * GDN_DK + GDN_WIDTH
CONV_K = 5
CHUNK = 64
MLA_HEADS = 8
MLA_NOPE = 128
MLA_ROPE = 64
MLA_DV = 128
MLA_WIDTH = MLA_HEADS * MLA_DV
Q_LORA = 512
KV_LORA = 256
ROPE_BASE = 10000.0
D_MIX = GDN_WIDTH + MLA_WIDTH
EPS = 1e-6
MLA_SCALE = (MLA_NOPE + MLA_ROPE) ** -0.5
GDN_SCALE = GDN_DK ** -0.5
LOG2_E = float(np.log2(np.e))

LANES = 128
SUBLANES = 8
VMEM_LIMIT = 56 * 1024 * 1024

COL_QKV = 0
COL_ZA = COL_QKV + GDN_QKV
COL_ZB = COL_ZA + GDN_WIDTH
COL_CQ = COL_ZB + MLA_WIDTH
COL_CKV = COL_CQ + Q_LORA
COL_KPE = COL_CKV + KV_LORA
COL_GATE = COL_KPE + 2 * MLA_ROPE
D_PROJ = COL_GATE + LANES
MISC_W = D_PROJ - COL_CQ
QK_PAD = 2 * LANES
TILE = 4 * CHUNK
CHUNK_LOG2 = CHUNK.bit_length() - 1
assert 1 << CHUNK_LOG2 == CHUNK and 2 * CHUNK == LANES

assert D_PROJ == 6144 and MISC_W == 1024 and COL_GATE - COL_CQ == 896


def _cparams(*sem):
    return pltpu.CompilerParams(dimension_semantics=sem, vmem_limit_bytes=VMEM_LIMIT)


def _silu(z):
    return z / (1.0 + jnp.exp(-z))


def _rms(x, g):
    return x * lax.rsqrt(jnp.mean(x * x, axis=-1, keepdims=True) + EPS) * g


def _inproj_kernel(x_ref, g_ref, w_ref, o_ref, h_ref):
    @pl.when(pl.program_id(1) == 0)
    def _():
        h_ref[...] = _rms(x_ref[...], g_ref[...]).astype(BF16)

    o_ref[...] = jnp.dot(h_ref[...], w_ref[...], preferred_element_type=F32)


def _inproj(x2d, g, w_pad, tm, tn):
    t = x2d.shape[0]
    return pl.pallas_call(
        _inproj_kernel,
        out_shape=jax.ShapeDtypeStruct((t, D_PROJ), F32),
        grid=(t // tm, D_PROJ // tn),
        in_specs=[
            pl.BlockSpec((tm, D_MODEL), lambda i, j: (i, 0)),
            pl.BlockSpec((1, D_MODEL), lambda i, j: (0, 0)),
            pl.BlockSpec((D_MODEL, tn), lambda i, j: (0, j)),
        ],
        out_specs=pl.BlockSpec((tm, tn), lambda i, j: (i, j)),
        scratch_shapes=[pltpu.VMEM((tm, D_MODEL), BF16)],
        compiler_params=_cparams("parallel", "arbitrary"),
        name="inproj",
    )(x2d, g, w_pad)


def _mla_proj_kernel(p_ref, gq_ref, gkv_ref, wq_ref, wkv_ref, cos_ref, sin_ref,
                     q_ref, k_ref, v_ref):
    t = p_ref[...]
    cqn = _rms(t[:, 0:Q_LORA], gq_ref[...]).astype(BF16)
    ckn = _rms(t[:, Q_LORA:Q_LORA + KV_LORA], gkv_ref[...]).astype(BF16)
    kpe2 = t[:, COL_KPE - COL_CQ:COL_GATE - COL_CQ]
    qe = jnp.dot(cqn, wq_ref[...], preferred_element_type=F32)
    kve = jnp.dot(ckn, wkv_ref[...], preferred_element_type=F32)
    c2 = cos_ref[...]
    s2 = sin_ref[...]
    krope = (kpe2 * c2 + pltpu.roll(kpe2, MLA_ROPE, axis=1) * s2).astype(BF16)
    for h in range(MLA_HEADS):
        qp = qe[:, h * QK_PAD + LANES:(h + 1) * QK_PAD]
        qr = qp * c2 + pltpu.roll(qp, MLA_ROPE, axis=1) * s2
        q_ref[0, h, :, 0:LANES] = qe[:, h * QK_PAD:h * QK_PAD + LANES].astype(BF16)
        q_ref[0, h, :, LANES:QK_PAD] = qr.astype(BF16)
        k_ref[0, h, :, 0:LANES] = kve[:, h * QK_PAD:h * QK_PAD + LANES].astype(BF16)
        k_ref[0, h, :, LANES:QK_PAD] = krope
        v_ref[0, h] = kve[:, h * QK_PAD + LANES:(h + 1) * QK_PAD].astype(BF16)


def _mla_proj(proj, gq, gkv, wq_ext, wkv, cos2, sin2, b, s, ts):
    nt = s // ts
    hs = MLA_HEADS
    return pl.pallas_call(
        _mla_proj_kernel,
        out_shape=(jax.ShapeDtypeStruct((b, hs, s, QK_PAD), BF16),
                   jax.ShapeDtypeStruct((b, hs, s, QK_PAD), BF16),
                   jax.ShapeDtypeStruct((b, hs, s, MLA_DV), BF16)),
        grid=(b, nt),
        in_specs=[
            pl.BlockSpec((ts, MISC_W), lambda bi, i: (bi * nt + i, COL_CQ // MISC_W)),
            pl.BlockSpec((1, Q_LORA), lambda bi, i: (0, 0)),
            pl.BlockSpec((1, KV_LORA), lambda bi, i: (0, 0)),
            pl.BlockSpec((Q_LORA, hs * QK_PAD), lambda bi, i: (0, 0)),
            pl.BlockSpec((KV_LORA, hs * QK_PAD), lambda bi, i: (0, 0)),
            pl.BlockSpec((ts, LANES), lambda bi, i: (i, 0)),
            pl.BlockSpec((ts, LANES), lambda bi, i: (i, 0)),
        ],
        out_specs=(pl.BlockSpec((1, hs, ts, QK_PAD), lambda bi, i: (bi, 0, i, 0)),
                   pl.BlockSpec((1, hs, ts, QK_PAD), lambda bi, i: (bi, 0, i, 0)),
                   pl.BlockSpec((1, hs, ts, MLA_DV), lambda bi, i: (bi, 0, i, 0))),
        compiler_params=_cparams("parallel", "parallel"),
        name="mla_proj",
    )(proj, gq, gkv, wq_ext, wkv, cos2, sin2)


def _attn_kernel(q_ref, k_ref, v_ref, z_ref, o_ref, sa_ref, sb_ref, p_ref, m_ref, l_ref, acc_ref,
                 *, tk, nk, rb):
    q = q_ref[0, 0]
    tq = q.shape[0]
    c2 = MLA_SCALE * LOG2_E
    nt_dims = (((1,), (1,)), ((), ()))

    def scores(j, s_ref):
        off = pl.multiple_of(j * tk, tk)
        s_ref[...] = lax.dot_general(q, k_ref[0, 0, pl.ds(off, tk), :], nt_dims,
                                     preferred_element_type=F32)

    def softmax_and_pv(j, s_ref):
        blocks = [slice(r * rb, (r + 1) * rb) for r in range(tq // rb)]
        lane_tiles = [slice(t * LANES, (t + 1) * LANES) for t in range(tk // LANES)]
        mx_parts = []
        for rows in blocks:
            mx = s_ref[rows, lane_tiles[0]]
            for lt in lane_tiles[1:]:
                mx = jnp.maximum(mx, s_ref[rows, lt])
            mx_parts.append(mx)
        mx_all = jnp.concatenate(mx_parts, axis=0)
        m_old = m_ref[...]
        row_max = jnp.broadcast_to(jnp.max(mx_all, axis=-1, keepdims=True), mx_all.shape)
        m_new = jnp.maximum(m_old, row_max * c2)
        alpha_all = jnp.exp2(m_old - m_new)
        m_ref[...] = m_new
        l_all = l_ref[...]
        l_parts = []
        for rows in blocks:
            m_b = m_new[rows]
            psum = None
            for lt in lane_tiles:
                p = jnp.exp2(s_ref[rows, lt] * c2 - m_b)
                psum = p if psum is None else psum + p
                p_ref[rows, lt] = p.astype(BF16)
            l_parts.append(alpha_all[rows] * l_all[rows] + psum)
        l_ref[...] = jnp.concatenate(l_parts, axis=0)
        off = pl.multiple_of(j * tk, tk)
        acc_ref[...] = acc_ref[...] * alpha_all + jnp.dot(
            p_ref[...], v_ref[0, 0, pl.ds(off, tk), :], preferred_element_type=F32)

    m_ref[...] = jnp.full(m_ref.shape, -jnp.inf, F32)
    l_ref[...] = jnp.zeros(l_ref.shape, F32)
    acc_ref[...] = jnp.zeros(acc_ref.shape, F32)
    scores(0, sa_ref)

    def body(jj, carry):
        j = 2 * jj
        scores(j + 1, sb_ref)
        softmax_and_pv(j, sa_ref)
        scores(j + 2, sa_ref)
        softmax_and_pv(j + 1, sb_ref)
        return carry

    n_pairs = (nk - 1) // 2
    lax.fori_loop(0, n_pairs, body, 0)
    j0 = 2 * n_pairs
    if nk - j0 == 2:
        scores(j0 + 1, sb_ref)
        softmax_and_pv(j0, sa_ref)
        softmax_and_pv(j0 + 1, sb_ref)
    else:
        softmax_and_pv(j0, sa_ref)
    l = jnp.sum(l_ref[...], axis=-1, keepdims=True)
    o_ref[0] = ((acc_ref[...] / l) * _silu(z_ref[...])).astype(BF16)


def _attention(q, k, v, proj, b, s, tq, tk):
    hs = MLA_HEADS
    nq = s // tq
    zb0 = COL_ZB // LANES
    return pl.pallas_call(
        functools.partial(_attn_kernel, tk=tk, nk=s // tk, rb=2 * SUBLANES),
        out_shape=jax.ShapeDtypeStruct((b, s, MLA_WIDTH), BF16),
        grid=(b, hs, nq),
        in_specs=[
            pl.BlockSpec((1, 1, tq, QK_PAD), lambda bi, h, i: (bi, h, i, 0)),
            pl.BlockSpec((1, 1, s, QK_PAD), lambda bi, h, i: (bi, h, 0, 0)),
            pl.BlockSpec((1, 1, s, MLA_DV), lambda bi, h, i: (bi, h, 0, 0)),
            pl.BlockSpec((tq, LANES), lambda bi, h, i: (bi * nq + i, zb0 + h)),
        ],
        out_specs=pl.BlockSpec((1, tq, MLA_DV), lambda bi, h, i: (bi, i, h)),
        scratch_shapes=[pltpu.VMEM((tq, tk), F32), pltpu.VMEM((tq, tk), F32),
                        pltpu.VMEM((tq, tk), BF16),
                        pltpu.VMEM((tq, LANES), F32), pltpu.VMEM((tq, LANES), F32),
                        pltpu.VMEM((tq, MLA_DV), F32)],
        compiler_params=_cparams("parallel", "parallel", "arbitrary"),
        name="attention",
    )(q, k, v, proj)


def _tile_masks():
    r = lax.broadcasted_iota(jnp.int32, (TILE, TILE), 0)
    c = lax.broadcasted_iota(jnp.int32, (TILE, TILE), 1)
    same = (r >> CHUNK_LOG2) == (c >> CHUNK_LOG2)
    return (same & (r >= c)).astype(F32), (same & (r <= c)).astype(F32)


def _col(x, lane):
    li = lax.broadcasted_iota(jnp.int32, x.shape, 1)
    col = jnp.sum(jnp.where(li == lane, x, 0.0), axis=1, keepdims=True)
    return jnp.broadcast_to(col, x.shape)


def _dup_chunks_t(cs):
    parts = []
    for c in range(TILE // CHUNK):
        blk = cs[c * CHUNK:(c + 1) * CHUNK, :]
        parts += [blk, blk]
    return jnp.concatenate(parts, axis=0).T


def _gdn_prep_kernel(q_ref, qp_ref, qn_ref, k_ref, kp_ref, kn_ref, v_ref, vp_ref, vn_ref,
                     cwq_ref, cwk_ref, cwv_ref, gate_ref, gpar_ref,
                     uf_ref, wf_ref, qdf_ref, kdf_ref, atf_ref,
                     ub_ref, wb_ref, qdb_ref, kdb_ref, atb_ref, egl_ref,
                     xs_ref, g_ref, csl_ref, csu_ref, cslt_ref, csut_ref):
    i = pl.program_id(1)
    nt = pl.num_programs(1)
    h = pl.program_id(2)

    @pl.when(h == 0)
    def _():
        t = gate_ref[...]
        a_log = gpar_ref[0:1, :]
        dt_b = gpar_ref[1:2, :]
        li = lax.broadcasted_iota(jnp.int32, t.shape, 1)
        xg = t + dt_b
        sp = jnp.maximum(xg, 0.0) + jnp.log1p(jnp.exp(-jnp.abs(xg)))
        gdec = -jnp.exp(a_log) * sp
        beta = 1.0 / (1.0 + jnp.exp(-t))
        n_gate = 2 * GDN_HEADS
        g = jnp.where(li < n_gate, beta, jnp.where(li < 2 * n_gate, gdec, 0.0))
        g_ref[...] = g
        tri_l, tri_u = _tile_masks()
        csl = jnp.dot(tri_l, g, preferred_element_type=F32, precision=lax.Precision.HIGHEST)
        csu = jnp.dot(tri_u, g, preferred_element_type=F32, precision=lax.Precision.HIGHEST)
        csl_ref[...] = csl
        csu_ref[...] = csu
        cslt_ref[...] = _dup_chunks_t(csl)
        csut_ref[...] = _dup_chunks_t(csu)

    def conv_silu(main_ref, prev_ref, next_ref, w_ref):
        xs_ref[0:SUBLANES, :] = jnp.where(i > 0, prev_ref[...], 0.0)
        xs_ref[SUBLANES:SUBLANES + TILE, :] = main_ref[...]
        xs_ref[SUBLANES + TILE:, :] = jnp.where(i < nt - 1, next_ref[...], 0.0)
        pad = (CONV_K - 1) // 2
        acc = xs_ref[pl.ds(SUBLANES - pad, TILE), :] * w_ref[0:1, :]
        for tap in range(1, CONV_K):
            acc = acc + xs_ref[pl.ds(SUBLANES - pad + tap, TILE), :] * w_ref[tap:tap + 1, :]
        return _silu(acc)

    def l2n(x):
        return x * lax.rsqrt(jnp.sum(x * x, axis=-1, keepdims=True) + EPS)

    q = l2n(conv_silu(q_ref, qp_ref, qn_ref, cwq_ref)) * GDN_SCALE
    k = l2n(conv_silu(k_ref, kp_ref, kn_ref, cwk_ref))
    v = conv_silu(v_ref, vp_ref, vn_ref, cwv_ref)
    qb = q.astype(BF16)
    kb16 = k.astype(BF16)

    ri = lax.broadcasted_iota(jnp.int32, (CHUNK, LANES), 0)
    li = lax.broadcasted_iota(jnp.int32, (CHUNK, LANES), 1)
    lj = li & (CHUNK - 1)
    left = li < CHUNK
    left_f = left.astype(F32)
    eye_left = (ri == li).astype(F32)
    zeros_w = jnp.zeros((CHUNK, LANES), BF16)
    zeros_r = jnp.zeros((CHUNK, 2 * LANES), BF16)
    nt_dims = (((1,), (1,)), ((), ()))
    nchunk = TILE // CHUNK

    g = g_ref[...]
    csl = csl_ref[...]
    csu = csu_ref[...]
    nh = GDN_HEADS
    grams = []
    for c in range(nchunk):
        rows = slice(c * CHUNK, (c + 1) * CHUNK)
        k2 = jnp.concatenate([kb16[rows], kb16[rows]], axis=0)
        grams.append((lax.dot_general(kb16[rows], k2, nt_dims, preferred_element_type=F32),
                      lax.dot_general(qb[rows], k2, nt_dims, preferred_element_type=F32)))

    def setup_direction(beta_lane, g_lane, cs, cs_other, cst_ref, tri, stri,
                        u_ref, w_ref, qd_ref, kd_ref, at_ref, egl_row):
        beta = _col(g, beta_lane)
        gc = _col(cs, g_lane)
        rest = _col(cs_other - g, g_lane)
        egc = jnp.exp(gc)
        qd_ref[0] = (q * egc).astype(BF16)
        kd_ref[0] = (k * jnp.exp(rest)).astype(BF16)
        vb = (v * beta).astype(BF16)
        kbg = (k * (beta * egc)).astype(BF16)
        tot = jnp.exp(gc + rest)
        gc_rows = cst_ref[pl.ds(g_lane, 1), :]
        chains = []
        for c in range(nchunk):
            rows = slice(c * CHUNK, (c + 1) * CHUNK)
            kk2, qk2 = grams[c]
            gc_row = gc_rows[:, c * LANES:(c + 1) * LANES]
            decay = jnp.exp(jnp.minimum(gc[rows] - gc_row, 0.0))
            x = jnp.where(left, eye_left, -(beta[rows] * kk2 * decay * stri))
            half = left_f if c % 2 == 0 else 1.0 - left_f
            at_ref[0, rows, :] = (qk2 * decay * (tri * half)).astype(BF16)
            egl_ref[0, c, pl.ds(egl_row, 1), :] = tot[c * CHUNK:c * CHUNK + 1, :]
            rhs = jnp.concatenate([jnp.concatenate([vb[rows], kbg[rows]], axis=1), zeros_r], axis=0)
            chains.append((x, rhs, rows, u_ref, w_ref))
        return chains

    lower = (ri >= lj).astype(F32)
    slower = (ri > lj).astype(F32)
    upper = (ri <= lj).astype(F32)
    supper = (ri < lj).astype(F32)
    chains = setup_direction(h, 2 * nh + h, csl, csu, cslt_ref, lower, slower,
                             uf_ref, wf_ref, qdf_ref, kdf_ref, atf_ref, h)
    chains += setup_direction(nh + h, 3 * nh + h, csu, csl, csut_ref, upper, supper,
                              ub_ref, wb_ref, qdb_ref, kdb_ref, atb_ref, nh + h)
    xs = [ch[0] for ch in chains]
    for _ in range(CHUNK_LOG2):
        nxt = []
        for x in xs:
            xb = x.astype(BF16)
            nxt.append(x * left_f + jnp.dot(xb, jnp.concatenate([zeros_w, xb], axis=0),
                                            preferred_element_type=F32))
        xs = nxt
    for x, (_, rhs, rows, u_ref, w_ref) in zip(xs, chains):
        uw = jnp.dot(x.astype(BF16), rhs, preferred_element_type=F32)
        u_ref[0, rows, :] = uw[:, 0:GDN_DV].astype(BF16)
        w_ref[0, rows, :] = uw[:, GDN_DV:].astype(BF16)


def _gdn_prep(proj, conv_w, gpar, b, s):
    nt = s // TILE
    hs = GDN_HEADS
    rows8 = TILE // SUBLANES
    last8 = b * s // SUBLANES - 1

    def main(col0):
        return pl.BlockSpec((TILE, LANES), lambda bi, i, h: (bi * nt + i, col0 + h))

    def prev(col0):
        return pl.BlockSpec((SUBLANES, LANES),
                            lambda bi, i, h: (jnp.maximum((bi * nt + i) * rows8 - 1, 0), col0 + h))

    def nxt(col0):
        return pl.BlockSpec((SUBLANES, LANES),
                            lambda bi, i, h: (jnp.minimum((bi * nt + i + 1) * rows8, last8), col0 + h))

    def cw(col0):
        return pl.BlockSpec((CONV_K, LANES), lambda bi, i, h: (0, col0 + h))

    qc, kc, vc = 0, hs, 2 * hs
    seq = jax.ShapeDtypeStruct((b, s, GDN_WIDTH), BF16)
    seq_spec = pl.BlockSpec((1, TILE, LANES), lambda bi, i, h: (bi, i, h))
    nchunk = TILE // CHUNK
    egl = jax.ShapeDtypeStruct((b * nt, nchunk, 2 * hs, LANES), F32)
    egl_spec = pl.BlockSpec((1, nchunk, 2 * hs, LANES), lambda bi, i, h: (bi * nt + i, 0, 0, 0))
    return pl.pallas_call(
        _gdn_prep_kernel,
        out_shape=(seq,) * 10 + (egl,),
        grid=(b, nt, hs),
        in_specs=[main(qc), prev(qc), nxt(qc), main(kc), prev(kc), nxt(kc),
                  main(vc), prev(vc), nxt(vc), cw(qc), cw(kc), cw(vc),
                  pl.BlockSpec((TILE, LANES), lambda bi, i, h: (bi * nt + i, COL_GATE // LANES)),
                  pl.BlockSpec((SUBLANES, LANES), lambda bi, i, h: (0, 0))],
        out_specs=(seq_spec,) * 10 + (egl_spec,),
        scratch_shapes=[pltpu.VMEM((TILE + 2 * SUBLANES, LANES), F32),
                        pltpu.VMEM((TILE, LANES), F32), pltpu.VMEM((TILE, LANES), F32),
                        pltpu.VMEM((TILE, LANES), F32), pltpu.VMEM((LANES, 2 * TILE), F32),
                        pltpu.VMEM((LANES, 2 * TILE), F32)],
        compiler_params=_cparams("parallel", "parallel", "arbitrary"),
        name="gdn_prep",
    )(*([proj] * 9), conv_w, conv_w, conv_w, proj, gpar)


def _gdn_scan_kernel(uf_ref, wf_ref, qdf_ref, kdf_ref, atf_ref, eglf_ref,
                     ub_ref, wb_ref, qdb_ref, kdb_ref, atb_ref, eglb_ref,
                     of_ref, ob_ref, st_ref):
    @pl.when(pl.program_id(1) == 0)
    def _():
        st_ref[...] = jnp.zeros_like(st_ref)

    nchunk = TILE // CHUNK
    tn_dims = (((0,), (0,)), ((), ()))

    def chain(idx, c, hd, u_ref, w_ref, qd_ref, kd_ref, at_ref, egl_ref, o_ref, egl_row):
        rows = pl.ds(pl.multiple_of(c * CHUNK, CHUNK), CHUNK)
        cols = slice(hd * LANES, (hd + 1) * LANES)
        state = st_ref[idx]
        sb = state.astype(BF16)
        ws = jnp.dot(w_ref[0, rows, cols], sb, preferred_element_type=F32)
        qs = jnp.dot(qd_ref[0, rows, cols], sb, preferred_element_type=F32)
        v_new = (u_ref[0, rows, cols].astype(F32) - ws).astype(BF16)
        v_pair = jnp.concatenate([v_new, v_new], axis=0)
        o_ref[0, rows, cols] = qs + jnp.dot(at_ref[0, rows, cols], v_pair,
                                            preferred_element_type=F32)
        decay = egl_ref[0, pl.ds(c, 1), egl_row, :]
        upd = lax.dot_general(kd_ref[0, rows, cols], v_new, tn_dims, preferred_element_type=F32)
        st_ref[idx] = state * decay + upd

    def step(c, carry):
        for hd in range(GDN_HEADS):
            chain(hd, c, hd, uf_ref, wf_ref, qdf_ref, kdf_ref, atf_ref, eglf_ref, of_ref, hd)
            chain(GDN_HEADS + hd, nchunk - 1 - c, hd, ub_ref, wb_ref, qdb_ref, kdb_ref, atb_ref,
                  eglb_ref, ob_ref, GDN_HEADS + hd)
        return carry

    lax.fori_loop(0, nchunk, step, 0)


def _gdn_scan(prep, b, s):
    uf, wf, qdf, kdf, atf, ub, wb, qdb, kdb, atb, egl = prep
    nt = s // TILE
    hs = GDN_HEADS
    nchunk = TILE // CHUNK
    fwd = pl.BlockSpec((1, TILE, GDN_WIDTH), lambda bi, i: (bi, i, 0))
    bwd = pl.BlockSpec((1, TILE, GDN_WIDTH), lambda bi, i: (bi, nt - 1 - i, 0))
    egl_f = pl.BlockSpec((1, nchunk, 2 * hs, LANES), lambda bi, i: (bi * nt + i, 0, 0, 0))
    egl_b = pl.BlockSpec((1, nchunk, 2 * hs, LANES), lambda bi, i: (bi * nt + nt - 1 - i, 0, 0, 0))
    out = jax.ShapeDtypeStruct((b, s, GDN_WIDTH), F32)
    return pl.pallas_call(
        _gdn_scan_kernel,
        out_shape=(out, out),
        grid=(b, nt),
        in_specs=[fwd] * 5 + [egl_f] + [bwd] * 5 + [egl_b],
        out_specs=(fwd, bwd),
        scratch_shapes=[pltpu.VMEM((2 * hs, GDN_DK, GDN_DV), F32)],
        compiler_params=_cparams("parallel", "arbitrary"),
        name="gdn_scan",
    )(uf, wf, qdf, kdf, atf, egl, ub, wb, qdb, kdb, atb, egl)


def _outproj_kernel(of_ref, ob_ref, za_ref, mixb_ref, x_ref, gn_ref, gp_ref, w_ref, y_ref):
    o = of_ref[...] + ob_ref[...]
    za = za_ref[...]
    parts = []
    for h in range(GDN_HEADS):
        sl = slice(h * GDN_DV, (h + 1) * GDN_DV)
        parts.append((_rms(o[:, sl], gn_ref[...]) * _silu(za[:, sl])).astype(BF16))
    mix_a = jnp.concatenate(parts, axis=1)
    y = jnp.dot(mix_a, w_ref[0:GDN_WIDTH, :], preferred_element_type=F32)
    y = y + jnp.dot(mixb_ref[...], w_ref[GDN_WIDTH:, :], preferred_element_type=F32)
    y_ref[...] = x_ref[...] + _rms(y, gp_ref[...])


def _outproj(o_f, o_b, proj, mix_b, x2d, gn, gp, w_out, tm):
    t = x2d.shape[0]
    row = lambda i: (i, 0)
    const = lambda i: (0, 0)
    return pl.pallas_call(
        _outproj_kernel,
        out_shape=jax.ShapeDtypeStruct((t, D_MODEL), F32),
        grid=(t // tm,),
        in_specs=[
            pl.BlockSpec((tm, GDN_WIDTH), row),
            pl.BlockSpec((tm, GDN_WIDTH), row),
            pl.BlockSpec((tm, GDN_WIDTH), lambda i: (i, COL_ZA // GDN_WIDTH)),
            pl.BlockSpec((tm, MLA_WIDTH), row),
            pl.BlockSpec((tm, D_MODEL), row),
            pl.BlockSpec((1, GDN_DV), const),
            pl.BlockSpec((1, D_MODEL), const),
            pl.BlockSpec((D_MIX, D_MODEL), const),
        ],
        out_specs=pl.BlockSpec((tm, D_MODEL), row),
        compiler_params=_cparams("parallel"),
        name="outproj",
    )(o_f, o_b, proj, mix_b, x2d, gn, gp, w_out)


def _swap_halves(w):
    half = MLA_ROPE // 2
    return jnp.concatenate([w[..., half:], w[..., :half]], axis=-1)


def _prep_layer(w_in, mla_w_uq, mla_w_ukv, w_out, a_log, dt_bias):
    o = np.cumsum((0, GDN_QKV, GDN_WIDTH, 2 * GDN_HEADS, 2 * GDN_HEADS, Q_LORA, KV_LORA,
                   MLA_ROPE, MLA_WIDTH))
    qkv, za, bl, al, cq, ckv, kpe, zb = (w_in[:, o[n]:o[n + 1]] for n in range(8))
    pad = jnp.zeros((D_MODEL, LANES - 4 * GDN_HEADS), w_in.dtype)
    w_pad = jnp.concatenate([qkv, za, zb, cq, ckv, kpe, _swap_halves(kpe), bl, al, pad],
                            axis=1).astype(BF16)
    wq = mla_w_uq.reshape(Q_LORA, MLA_HEADS, MLA_NOPE + MLA_ROPE)
    wq_ext = jnp.concatenate([wq, _swap_halves(wq[..., MLA_NOPE:])], axis=-1)
    wq_ext = wq_ext.reshape(Q_LORA, MLA_HEADS * QK_PAD).astype(BF16)
    wkv = mla_w_ukv.astype(BF16)
    gpar = jnp.zeros((SUBLANES, LANES), F32)
    n_gate = 2 * GDN_HEADS
    gpar = gpar.at[0, n_gate:2 * n_gate].set(a_log.reshape(-1))
    gpar = gpar.at[1, n_gate:2 * n_gate].set(dt_bias.reshape(-1))
    return w_pad, wq_ext, wkv, w_out.astype(BF16), gpar


def _rope_tables(s):
    pos = jnp.arange(s, dtype=F32)
    inv = ROPE_BASE ** (-jnp.arange(0, MLA_ROPE, 2, dtype=F32) / MLA_ROPE)
    ang = pos[:, None] * inv[None, :]
    cos, sin = jnp.cos(ang), jnp.sin(ang)
    zero = jnp.zeros((s, LANES - MLA_ROPE), F32)
    return (jnp.concatenate([cos, cos, zero], axis=1),
            jnp.concatenate([-sin, sin, zero], axis=1))


def _pick(n, prefs):
    for p in prefs:
        if n % p == 0:
            return p
    return n


def _tiles(b, s):
    t = b * s
    return dict(tm_in=_pick(t, (1024, 512, 256)), tn_in=1024,
                ts_mla=_pick(s, (512, 256)),
                tq=_pick(s, (512, 256, 128)), tk=_pick(s, (1024, 512, 256, 128)),
                tm_out=_pick(t, (512, 256)))


def _layer(x2d, b, s, tl, pre_g, post_g, conv_w, gdn_norm_g, q_norm_g, kv_norm_g,
           w_pad, wq_ext, wkv, w_out, gpar, cos2, sin2):
    proj = _inproj(x2d, pre_g.reshape(1, -1), w_pad, tl["tm_in"], tl["tn_in"])
    prep = _gdn_prep(proj, conv_w, gpar, b, s)
    o_f, o_b = _gdn_scan(prep, b, s)
    q, k, v = _mla_proj(proj, q_norm_g.reshape(1, -1), kv_norm_g.reshape(1, -1), wq_ext, wkv,
                        cos2, sin2, b, s, tl["ts_mla"])
    mix_b = _attention(q, k, v, proj, b, s, tl["tq"], tl["tk"])
    return _outproj(o_f.reshape(b * s, -1), o_b.reshape(b * s, -1), proj,
                    mix_b.reshape(b * s, -1), x2d, gdn_norm_g.reshape(1, -1),
                    post_g.reshape(1, -1), w_out, tl["tm_out"])


def _trunk(x, layers, pre_norm_g, post_norm_g, conv_w, gdn_norm_g, mla_q_norm_g, mla_kv_norm_g):
    b, s, d = x.shape
    assert d == D_MODEL and s % TILE == 0
    tl = _tiles(b, s)
    cos2, sin2 = _rope_tables(s)
    x2d = x.reshape(b * s, d)
    for l in range(DEPTH):
        x2d = _layer(x2d, b, s, tl, pre_norm_g[l], post_norm_g[l], conv_w[l], gdn_norm_g[l],
                     mla_q_norm_g[l], mla_kv_norm_g[l], *layers[l], cos2, sin2)
    return x2d.reshape(b, s, d)


def kernel(x_prompt, x_sample, pre_norm_g, post_norm_g, w_in, conv_w, gdn_a_log, gdn_dt_bias,
           gdn_norm_g, mla_q_norm_g, mla_kv_norm_g, mla_w_uq, mla_w_ukv, w_out):
    layers = [_prep_layer(w_in[l], mla_w_uq[l], mla_w_ukv[l], w_out[l], gdn_a_log[l],
                          gdn_dt_bias[l]) for l in range(DEPTH)]
    args = (layers, pre_norm_g, post_norm_g, conv_w, gdn_norm_g, mla_q_norm_g, mla_kv_norm_g)
    return (_trunk(x_prompt, *args), _trunk(x_sample, *args))
```

```python
import functools

import numpy as np
import jax
import jax.numpy as jnp
from jax import lax
from jax.experimental import pallas as pl
from jax.experimental.pallas import tpu as pltpu

F32 = jnp.float32
BF16 = jnp.bfloat16

D_MODEL = 2048
DEPTH = 2
GDN_HEADS = 8
GDN_DK = 128
GDN_DV = 128
GDN_WIDTH = GDN_HEADS * GDN_DV
GDN_QKV = 2 * GDN_HEADS * GDN_DK + GDN_WIDTH
CONV_K = 5
CHUNK = 64
MLA_HEADS = 8
MLA_NOPE = 128
MLA_ROPE = 64
MLA_DV = 128
MLA_WIDTH = MLA_HEADS * MLA_DV
Q_LORA = 512
KV_LORA = 256
ROPE_BASE = 10000.0
D_MIX = GDN_WIDTH + MLA_WIDTH
EPS = 1e-6
MLA_SCALE = (MLA_NOPE + MLA_ROPE) ** -0.5
GDN_SCALE = GDN_DK ** -0.5
LOG2_E = float(np.log2(np.e))

LANES = 128
SUBLANES = 8
VMEM_LIMIT = 56 * 1024 * 1024

COL_QKV = 0
COL_ZA = COL_QKV + GDN_QKV
COL_ZB = COL_ZA + GDN_WIDTH
COL_CQ = COL_ZB + MLA_WIDTH
COL_CKV = COL_CQ + Q_LORA
COL_KPE = COL_CKV + KV_LORA
COL_GATE = COL_KPE + 2 * MLA_ROPE
D_PROJ = COL_GATE + LANES
MISC_W = D_PROJ - COL_CQ
QK_PAD = 2 * LANES
TILE = 4 * CHUNK
CHUNK_LOG2 = CHUNK.bit_length() - 1
assert 1 << CHUNK_LOG2 == CHUNK and 2 * CHUNK == LANES

assert D_PROJ == 6144 and MISC_W == 1024 and COL_GATE - COL_CQ == 896


def _cparams(*sem):
    return pltpu.CompilerParams(dimension_semantics=sem, vmem_limit_bytes=VMEM_LIMIT)


def _silu(z):
    return z / (1.0 + jnp.exp(-z))


def _rms(x, g):
    return x * lax.rsqrt(jnp.mean(x * x, axis=-1, keepdims=True) + EPS) * g


def _inproj_kernel(x_ref, g_ref, w_ref, o_ref, h_ref):
    @pl.when(pl.program_id(1) == 0)
    def _():
        h_ref[...] = _rms(x_ref[...], g_ref[...]).astype(BF16)

    o_ref[...] = jnp.dot(h_ref[...], w_ref[...], preferred_element_type=F32)


def _inproj(x2d, g, w_pad, tm, tn):
    t = x2d.shape[0]
    return pl.pallas_call(
        _inproj_kernel,
        out_shape=jax.ShapeDtypeStruct((t, D_PROJ), F32),
        grid=(t // tm, D_PROJ // tn),
        in_specs=[
            pl.BlockSpec((tm, D_MODEL), lambda i, j: (i, 0)),
            pl.BlockSpec((1, D_MODEL), lambda i, j: (0, 0)),
            pl.BlockSpec((D_MODEL, tn), lambda i, j: (0, j)),
        ],
        out_specs=pl.BlockSpec((tm, tn), lambda i, j: (i, j)),
        scratch_shapes=[pltpu.VMEM((tm, D_MODEL), BF16)],
        compiler_params=_cparams("parallel", "arbitrary"),
        name="inproj",
    )(x2d, g, w_pad)


def _mla_proj_kernel(p_ref, gq_ref, gkv_ref, wq_ref, wkv_ref, cos_ref, sin_ref,
                     q_ref, k_ref, v_ref):
    t = p_ref[...]
    cqn = _rms(t[:, 0:Q_LORA], gq_ref[...]).astype(BF16)
    ckn = _rms(t[:, Q_LORA:Q_LORA + KV_LORA], gkv_ref[...]).astype(BF16)
    kpe2 = t[:, COL_KPE - COL_CQ:COL_GATE - COL_CQ]
    qe = jnp.dot(cqn, wq_ref[...], preferred_element_type=F32)
    kve = jnp.dot(ckn, wkv_ref[...], preferred_element_type=F32)
    c2 = cos_ref[...]
    s2 = sin_ref[...]
    krope = (kpe2 * c2 + pltpu.roll(kpe2, MLA_ROPE, axis=1) * s2).astype(BF16)
    for h in range(MLA_HEADS):
        qp = qe[:, h * QK_PAD + LANES:(h + 1) * QK_PAD]
        qr = qp * c2 + pltpu.roll(qp, MLA_ROPE, axis=1) * s2
        q_ref[0, h, :, 0:LANES] = qe[:, h * QK_PAD:h * QK_PAD + LANES].astype(BF16)
        q_ref[0, h, :, LANES:QK_PAD] = qr.astype(BF16)
        k_ref[0, h, :, 0:LANES] = kve[:, h * QK_PAD:h * QK_PAD + LANES].astype(BF16)
        k_ref[0, h, :, LANES:QK_PAD] = krope
        v_ref[0, h] = kve[:, h * QK_PAD + LANES:(h + 1) * QK_PAD].astype(BF16)


def _mla_proj(proj, gq, gkv, wq_ext, wkv, cos2, sin2, b, s, ts):
    nt = s // ts
    hs = MLA_HEADS
    return pl.pallas_call(
        _mla_proj_kernel,
        out_shape=(jax.ShapeDtypeStruct((b, hs, s, QK_PAD), BF16),
                   jax.ShapeDtypeStruct((b, hs, s, QK_PAD), BF16),
                   jax.ShapeDtypeStruct((b, hs, s, MLA_DV), BF16)),
        grid=(b, nt),
        in_specs=[
            pl.BlockSpec((ts, MISC_W), lambda bi, i: (bi * nt + i, COL_CQ // MISC_W)),
            pl.BlockSpec((1, Q_LORA), lambda bi, i: (0, 0)),
            pl.BlockSpec((1, KV_LORA), lambda bi, i: (0, 0)),
            pl.BlockSpec((Q_LORA, hs * QK_PAD), lambda bi, i: (0, 0)),
            pl.BlockSpec((KV_LORA, hs * QK_PAD), lambda bi, i: (0, 0)),
            pl.BlockSpec((ts, LANES), lambda bi, i: (i, 0)),
            pl.BlockSpec((ts, LANES), lambda bi, i: (i, 0)),
        ],
        out_specs=(pl.BlockSpec((1, hs, ts, QK_PAD), lambda bi, i: (bi, 0, i, 0)),
                   pl.BlockSpec((1, hs, ts, QK_PAD), lambda bi, i: (bi, 0, i, 0)),
                   pl.BlockSpec((1, hs, ts, MLA_DV), lambda bi, i: (bi, 0, i, 0))),
        compiler_params=_cparams("parallel", "parallel"),
        name="mla_proj",
    )(proj, gq, gkv, wq_ext, wkv, cos2, sin2)


def _attn_kernel(q_ref, k_ref, v_ref, z_ref, o_ref, sa_ref, sb_ref, p_ref, m_ref, l_ref, acc_ref,
                 *, tk, nk, rb):
    q = q_ref[0, 0]
    tq = q.shape[0]
    c2 = MLA_SCALE * LOG2_E
    nt_dims = (((1,), (1,)), ((), ()))

    def scores(j, s_ref):
        off = pl.multiple_of(j * tk, tk)
        s_ref[...] = lax.dot_general(q, k_ref[0, 0, pl.ds(off, tk), :], nt_dims,
                                     preferred_element_type=F32)

    def softmax_and_pv(j, s_ref):
        blocks = [slice(r * rb, (r + 1) * rb) for r in range(tq // rb)]
        lane_tiles = [slice(t * LANES, (t + 1) * LANES) for t in range(tk // LANES)]
        mx_parts = []
        for rows in blocks:
            mx = s_ref[rows, lane_tiles[0]]
            for lt in lane_tiles[1:]:
                mx = jnp.maximum(mx, s_ref[rows, lt])
            mx_parts.append(mx)
        mx_all = jnp.concatenate(mx_parts, axis=0)
        m_old = m_ref[...]
        row_max = jnp.broadcast_to(jnp.max(mx_all, axis=-1, keepdims=True), mx_all.shape)
        m_new = jnp.maximum(m_old, row_max * c2)
        alpha_all = jnp.exp2(m_old - m_new)
        m_ref[...] = m_new
        l_all = l_ref[...]
        l_parts = []
        for rows in blocks:
            m_b = m_new[rows]
            psum = None
            for lt in lane_tiles:
                p = jnp.exp2(s_ref[rows, lt] * c2 - m_b)
                psum = p if psum is None else psum + p
                p_ref[rows, lt] = p.astype(BF16)
            l_parts.append(alpha_all[rows] * l_all[rows] + psum)
        l_ref[...] = jnp.concatenate(l_parts, axis=0)
        off = pl.multiple_of(j * tk, tk)
        acc_ref[...] = acc_ref[...] * alpha_all + jnp.dot(
            p_ref[...], v_ref[0, 0, pl.ds(off, tk), :], preferred_element_type=F32)

    m_ref[...] = jnp.full(m_ref.shape, -jnp.inf, F32)
    l_ref[...] = jnp.zeros(l_ref.shape, F32)
    acc_ref[...] = jnp.zeros(acc_ref.shape, F32)
    scores(0, sa_ref)

    def body(jj, carry):
        j = 2 * jj
        scores(j + 1, sb_ref)
        softmax_and_pv(j, sa_ref)
        scores(j + 2, sa_ref)
        softmax_and_pv(j + 1, sb_ref)
        return carry

    n_pairs = (nk - 1) // 2
    lax.fori_loop(0, n_pairs, body, 0)
    j0 = 2 * n_pairs
    if nk - j0 == 2:
        scores(j0 + 1, sb_ref)
        softmax_and_pv(j0, sa_ref)
        softmax_and_pv(j0 + 1, sb_ref)
    else:
        softmax_and_pv(j0, sa_ref)
    l = jnp.sum(l_ref[...], axis=-1, keepdims=True)
    o_ref[0] = ((acc_ref[...] / l) * _silu(z_ref[...])).astype(BF16)


def _attention(q, k, v, proj, b, s, tq, tk):
    hs = MLA_HEADS
    nq = s // tq
    zb0 = COL_ZB // LANES
    return pl.pallas_call(
        functools.partial(_attn_kernel, tk=tk, nk=s // tk, rb=2 * SUBLANES),
        out_shape=jax.ShapeDtypeStruct((b, s, MLA_WIDTH), BF16),
        grid=(b, hs, nq),
        in_specs=[
            pl.BlockSpec((1, 1, tq, QK_PAD), lambda bi, h, i: (bi, h, i, 0)),
            pl.BlockSpec((1, 1, s, QK_PAD), lambda bi, h, i: (bi, h, 0, 0)),
            pl.BlockSpec((1, 1, s, MLA_DV), lambda bi, h, i: (bi, h, 0, 0)),
            pl.BlockSpec((tq, LANES), lambda bi, h, i: (bi * nq + i, zb0 + h)),
        ],
        out_specs=pl.BlockSpec((1, tq, MLA_DV), lambda bi, h, i: (bi, i, h)),
        scratch_shapes=[pltpu.VMEM((tq, tk), F32), pltpu.VMEM((tq, tk), F32),
                        pltpu.VMEM((tq, tk), BF16),
                        pltpu.VMEM((tq, LANES), F32), pltpu.VMEM((tq, LANES), F32),
                        pltpu.VMEM((tq, MLA_DV), F32)],
        compiler_params=_cparams("parallel", "parallel", "arbitrary"),
        name="attention",
    )(q, k, v, proj)


def _tile_masks():
    r = lax.broadcasted_iota(jnp.int32, (TILE, TILE), 0)
    c = lax.broadcasted_iota(jnp.int32, (TILE, TILE), 1)
    same = (r >> CHUNK_LOG2) == (c >> CHUNK_LOG2)
    return (same & (r >= c)).astype(F32), (same & (r <= c)).astype(F32)


def _col(x, lane):
    li = lax.broadcasted_iota(jnp.int32, x.shape, 1)
    col = jnp.sum(jnp.where(li == lane, x, 0.0), axis=1, keepdims=True)
    return jnp.broadcast_to(col, x.shape)


def _dup_chunks_t(cs):
    parts = []
    for c in range(TILE // CHUNK):
        blk = cs[c * CHUNK:(c + 1) * CHUNK, :]
        parts += [blk, blk]
    return jnp.concatenate(parts, axis=0).T


def _gdn_prep_kernel(q_ref, qp_ref, qn_ref, k_ref, kp_ref, kn_ref, v_ref, vp_ref, vn_ref,
                     cwq_ref, cwk_ref, cwv_ref, gate_ref, gpar_ref,
                     uf_ref, wf_ref, qdf_ref, kdf_ref, atf_ref,
                     ub_ref, wb_ref, qdb_ref, kdb_ref, atb_ref, egl_ref,
                     xs_ref, g_ref, csl_ref, csu_ref, cslt_ref, csut_ref):
    i = pl.program_id(1)
    nt = pl.num_programs(1)
    hg = pl.program_id(2)
    hps = q_ref.shape[1] // LANES

    @pl.when(hg == 0)
    def _():
        t = gate_ref[...]
        a_log = gpar_ref[0:1, :]
        dt_b = gpar_ref[1:2, :]
        li = lax.broadcasted_iota(jnp.int32, t.shape, 1)
        xg = t + dt_b
        sp = jnp.maximum(xg, 0.0) + jnp.log1p(jnp.exp(-jnp.abs(xg)))
        gdec = -jnp.exp(a_log) * sp
        beta = 1.0 / (1.0 + jnp.exp(-t))
        n_gate = 2 * GDN_HEADS
        g = jnp.where(li < n_gate, beta, jnp.where(li < 2 * n_gate, gdec, 0.0))
        g_ref[...] = g
        tri_l, tri_u = _tile_masks()
        csl = jnp.dot(tri_l, g, preferred_element_type=F32, precision=lax.Precision.HIGHEST)
        csu = jnp.dot(tri_u, g, preferred_element_type=F32, precision=lax.Precision.HIGHEST)
        csl_ref[...] = csl
        csu_ref[...] = csu
        cslt_ref[...] = _dup_chunks_t(csl)
        csut_ref[...] = _dup_chunks_t(csu)

    def conv_silu(main_ref, prev_ref, next_ref, w_ref):
        xs_ref[0:SUBLANES, :] = jnp.where(i > 0, prev_ref[...], 0.0)
        xs_ref[SUBLANES:SUBLANES + TILE, :] = main_ref[...]
        xs_ref[SUBLANES + TILE:, :] = jnp.where(i < nt - 1, next_ref[...], 0.0)
        pad = (CONV_K - 1) // 2
        xs = xs_ref[...]
        n = xs.shape[0]
        acc = None
        for tap in range(CONV_K):
            sh = xs if tap == pad else pltpu.roll(xs, (pad - tap) % n, axis=0)
            term = sh[SUBLANES:SUBLANES + TILE, :] * w_ref[tap:tap + 1, :]
            acc = term if acc is None else acc + term
        return _silu(acc)

    def l2n(x):
        return x * lax.rsqrt(jnp.sum(x * x, axis=-1, keepdims=True) + EPS)

    q_all = conv_silu(q_ref, qp_ref, qn_ref, cwq_ref)
    k_all = conv_silu(k_ref, kp_ref, kn_ref, cwk_ref)
    v_all = conv_silu(v_ref, vp_ref, vn_ref, cwv_ref)

    ri = lax.broadcasted_iota(jnp.int32, (CHUNK, LANES), 0)
    li = lax.broadcasted_iota(jnp.int32, (CHUNK, LANES), 1)
    lj = li & (CHUNK - 1)
    left = li < CHUNK
    left_f = left.astype(F32)
    eye_left = (ri == li).astype(F32)
    zeros_w = jnp.zeros((CHUNK, LANES), BF16)
    zeros_r = jnp.zeros((CHUNK, 2 * LANES), BF16)
    nt_dims = (((1,), (1,)), ((), ()))
    nchunk = TILE // CHUNK

    g = g_ref[...]
    csl = csl_ref[...]
    csu = csu_ref[...]
    nh = GDN_HEADS
    lower = (ri >= lj).astype(F32)
    slower = (ri > lj).astype(F32)
    upper = (ri <= lj).astype(F32)
    supper = (ri < lj).astype(F32)

    def setup_head(hh):
        h = hg * hps + hh
        cols = slice(hh * LANES, (hh + 1) * LANES)
        q = l2n(q_all[:, cols]) * GDN_SCALE
        k = l2n(k_all[:, cols])
        v = v_all[:, cols]
        qb = q.astype(BF16)
        kb16 = k.astype(BF16)
        grams = []
        for c in range(nchunk):
            rows = slice(c * CHUNK, (c + 1) * CHUNK)
            k2 = jnp.concatenate([kb16[rows], kb16[rows]], axis=0)
            grams.append((lax.dot_general(kb16[rows], k2, nt_dims, preferred_element_type=F32),
                          lax.dot_general(qb[rows], k2, nt_dims, preferred_element_type=F32)))

        def setup_direction(beta_lane, g_lane, cs, cs_other, cst_ref, tri, stri,
                            u_ref, w_ref, qd_ref, kd_ref, at_ref, egl_row):
            beta = _col(g, beta_lane)
            gc = _col(cs, g_lane)
            rest = _col(cs_other - g, g_lane)
            egc = jnp.exp(gc)
            qd_ref[0, :, cols] = (q * egc).astype(BF16)
            kd_ref[0, :, cols] = (k * jnp.exp(rest)).astype(BF16)
            vb = (v * beta).astype(BF16)
            kbg = (k * (beta * egc)).astype(BF16)
            tot = jnp.exp(gc + rest)
            gc_rows = cst_ref[pl.ds(g_lane, 1), :]
            chains = []
            for c in range(nchunk):
                rows = slice(c * CHUNK, (c + 1) * CHUNK)
                kk2, qk2 = grams[c]
                gc_row = gc_rows[:, c * LANES:(c + 1) * LANES]
                decay = jnp.exp(jnp.minimum(gc[rows] - gc_row, 0.0))
                x = jnp.where(left, eye_left, -(beta[rows] * kk2 * decay * stri))
                half = left_f if c % 2 == 0 else 1.0 - left_f
                at_ref[0, rows, cols] = (qk2 * decay * (tri * half)).astype(BF16)
                egl_ref[0, c, pl.ds(egl_row, 1), :] = tot[c * CHUNK:c * CHUNK + 1, :]
                rhs = jnp.concatenate(
                    [jnp.concatenate([vb[rows], kbg[rows]], axis=1), zeros_r], axis=0)
                chains.append((x, rhs, rows, cols, u_ref, w_ref))
            return chains

        return (setup_direction(h, 2 * nh + h, csl, csu, cslt_ref, lower, slower,
                                uf_ref, wf_ref, qdf_ref, kdf_ref, atf_ref, h)
                + setup_direction(nh + h, 3 * nh + h, csu, csl, csut_ref, upper, supper,
                                  ub_ref, wb_ref, qdb_ref, kdb_ref, atb_ref, nh + h))

    chains = []
    for hh in range(hps):
        chains += setup_head(hh)
    xs = [ch[0] for ch in chains]
    for _ in range(CHUNK_LOG2):
        nxt = []
        for x in xs:
            xb = x.astype(BF16)
            nxt.append(x * left_f + jnp.dot(xb, jnp.concatenate([zeros_w, xb], axis=0),
                                            preferred_element_type=F32))
        xs = nxt
    for x, (_, rhs, rows, cols, u_ref, w_ref) in zip(xs, chains):
        uw = jnp.dot(x.astype(BF16), rhs, preferred_element_type=F32)
        u_ref[0, rows, cols] = uw[:, 0:GDN_DV].astype(BF16)
        w_ref[0, rows, cols] = uw[:, GDN_DV:].astype(BF16)


def _gdn_prep(proj, conv_w, gpar, b, s, hps):
    nt = s // TILE
    hs = GDN_HEADS
    rows8 = TILE // SUBLANES
    last8 = b * s // SUBLANES - 1
    wid = hps * LANES
    ng = hs // hps

    def main(col0):
        return pl.BlockSpec((TILE, wid), lambda bi, i, h: (bi * nt + i, col0 + h))

    def prev(col0):
        return pl.BlockSpec((SUBLANES, wid),
                            lambda bi, i, h: (jnp.maximum((bi * nt + i) * rows8 - 1, 0), col0 + h))

    def nxt(col0):
        return pl.BlockSpec((SUBLANES, wid),
                            lambda bi, i, h: (jnp.minimum((bi * nt + i + 1) * rows8, last8), col0 + h))

    def cw(col0):
        return pl.BlockSpec((CONV_K, wid), lambda bi, i, h: (0, col0 + h))

    qc, kc, vc = 0, ng, 2 * ng
    seq = jax.ShapeDtypeStruct((b, s, GDN_WIDTH), BF16)
    seq_spec = pl.BlockSpec((1, TILE, wid), lambda bi, i, h: (bi, i, h))
    nchunk = TILE // CHUNK
    egl = jax.ShapeDtypeStruct((b * nt, nchunk, 2 * hs, LANES), F32)
    egl_spec = pl.BlockSpec((1, nchunk, 2 * hs, LANES), lambda bi, i, h: (bi * nt + i, 0, 0, 0))
    return pl.pallas_call(
        _gdn_prep_kernel,
        out_shape=(seq,) * 10 + (egl,),
        grid=(b, nt, ng),
        in_specs=[main(qc), prev(qc), nxt(qc), main(kc), prev(kc), nxt(kc),
                  main(vc), prev(vc), nxt(vc), cw(qc), cw(kc), cw(vc),
                  pl.BlockSpec((TILE, LANES), lambda bi, i, h: (bi * nt + i, COL_GATE // LANES)),
                  pl.BlockSpec((SUBLANES, LANES), lambda bi, i, h: (0, 0))],
        out_specs=(seq_spec,) * 10 + (egl_spec,),
        scratch_shapes=[pltpu.VMEM((TILE + 2 * SUBLANES, wid), F32),
                        pltpu.VMEM((TILE, LANES), F32), pltpu.VMEM((TILE, LANES), F32),
                        pltpu.VMEM((TILE, LANES), F32), pltpu.VMEM((LANES, 2 * TILE), F32),
                        pltpu.VMEM((LANES, 2 * TILE), F32)],
        compiler_params=_cparams("parallel", "parallel", "arbitrary"),
        name="gdn_prep",
    )(*([proj] * 9), conv_w, conv_w, conv_w, proj, gpar)


def _gdn_scan_kernel(uf_ref, wf_ref, qdf_ref, kdf_ref, atf_ref, eglf_ref,
                     ub_ref, wb_ref, qdb_ref, kdb_ref, atb_ref, eglb_ref,
                     of_ref, ob_ref, st_ref):
    @pl.when(pl.program_id(1) == 0)
    def _():
        st_ref[...] = jnp.zeros_like(st_ref)

    nchunk = TILE // CHUNK
    tn_dims = (((0,), (0,)), ((), ()))

    fwd_refs = (uf_ref, wf_ref, qdf_ref, kdf_ref, atf_ref, eglf_ref, of_ref)
    bwd_refs = (ub_ref, wb_ref, qdb_ref, kdb_ref, atb_ref, eglb_ref, ob_ref)

    def step(c, carry):
        chains = []
        for hd in range(GDN_HEADS):
            chains.append((hd, c, hd, fwd_refs))
            chains.append((GDN_HEADS + hd, nchunk - 1 - c, hd, bwd_refs))
        stage1 = []
        for idx, cc, hd, (u_ref, w_ref, qd_ref, kd_ref, at_ref, egl_ref, o_ref) in chains:
            rows = pl.ds(pl.multiple_of(cc * CHUNK, CHUNK), CHUNK)
            cols = slice(hd * LANES, (hd + 1) * LANES)
            sb = st_ref[idx].astype(BF16)
            ws = jnp.dot(w_ref[0, rows, cols], sb, preferred_element_type=F32)
            qs = jnp.dot(qd_ref[0, rows, cols], sb, preferred_element_type=F32)
            stage1.append((rows, cols, ws, qs))
        for (idx, cc, hd, refs), (rows, cols, ws, qs) in zip(chains, stage1):
            u_ref, w_ref, qd_ref, kd_ref, at_ref, egl_ref, o_ref = refs
            v_new = (u_ref[0, rows, cols].astype(F32) - ws).astype(BF16)
            v_pair = jnp.concatenate([v_new, v_new], axis=0)
            o_ref[0, rows, cols] = qs + jnp.dot(at_ref[0, rows, cols], v_pair,
                                                preferred_element_type=F32)
            decay = egl_ref[0, pl.ds(cc, 1), idx, :]
            upd = lax.dot_general(kd_ref[0, rows, cols], v_new, tn_dims,
                                  preferred_element_type=F32)
            st_ref[idx] = st_ref[idx] * decay + upd
        return carry

    lax.fori_loop(0, nchunk, step, 0)


def _gdn_scan(prep, b, s):
    uf, wf, qdf, kdf, atf, ub, wb, qdb, kdb, atb, egl = prep
    nt = s // TILE
    hs = GDN_HEADS
    nchunk = TILE // CHUNK
    fwd = pl.BlockSpec((1, TILE, GDN_WIDTH), lambda bi, i: (bi, i, 0))
    bwd = pl.BlockSpec((1, TILE, GDN_WIDTH), lambda bi, i: (bi, nt - 1 - i, 0))
    egl_f = pl.BlockSpec((1, nchunk, 2 * hs, LANES), lambda bi, i: (bi * nt + i, 0, 0, 0))
    egl_b = pl.BlockSpec((1, nchunk, 2 * hs, LANES), lambda bi, i: (bi * nt + nt - 1 - i, 0, 0, 0))
    out = jax.ShapeDtypeStruct((b, s, GDN_WIDTH), F32)
    return pl.pallas_call(
        _gdn_scan_kernel,
        out_shape=(out, out),
        grid=(b, nt),
        in_specs=[fwd] * 5 + [egl_f] + [bwd] * 5 + [egl_b],
        out_specs=(fwd, bwd),
        scratch_shapes=[pltpu.VMEM((2 * hs, GDN_DK, GDN_DV), F32)],
        compiler_params=_cparams("parallel", "arbitrary"),
        name="gdn_scan",
    )(uf, wf, qdf, kdf, atf, egl, ub, wb, qdb, kdb, atb, egl)


def _outproj_kernel(of_ref, ob_ref, za_ref, mixb_ref, x_ref, gn_ref, gp_ref, w_ref, y_ref):
    o = of_ref[...] + ob_ref[...]
    za = za_ref[...]
    parts = []
    for h in range(GDN_HEADS):
        sl = slice(h * GDN_DV, (h + 1) * GDN_DV)
        parts.append((_rms(o[:, sl], gn_ref[...]) * _silu(za[:, sl])).astype(BF16))
    mix_a = jnp.concatenate(parts, axis=1)
    y = jnp.dot(mix_a, w_ref[0:GDN_WIDTH, :], preferred_element_type=F32)
    y = y + jnp.dot(mixb_ref[...], w_ref[GDN_WIDTH:, :], preferred_element_type=F32)
    y_ref[...] = x_ref[...] + _rms(y, gp_ref[...])


def _outproj(o_f, o_b, proj, mix_b, x2d, gn, gp, w_out, tm):
    t = x2d.shape[0]
    row = lambda i: (i, 0)
    const = lambda i: (0, 0)
    return pl.pallas_call(
        _outproj_kernel,
        out_shape=jax.ShapeDtypeStruct((t, D_MODEL), F32),
        grid=(t // tm,),
        in_specs=[
            pl.BlockSpec((tm, GDN_WIDTH), row),
            pl.BlockSpec((tm, GDN_WIDTH), row),
            pl.BlockSpec((tm, GDN_WIDTH), lambda i: (i, COL_ZA // GDN_WIDTH)),
            pl.BlockSpec((tm, MLA_WIDTH), row),
            pl.BlockSpec((tm, D_MODEL), row),
            pl.BlockSpec((1, GDN_DV), const),
            pl.BlockSpec((1, D_MODEL), const),
            pl.BlockSpec((D_MIX, D_MODEL), const),
        ],
        out_specs=pl.BlockSpec((tm, D_MODEL), row),
        compiler_params=_cparams("parallel"),
        name="outproj",
    )(o_f, o_b, proj, mix_b, x2d, gn, gp, w_out)


def _swap_halves(w):
    half = MLA_ROPE // 2
    return jnp.concatenate([w[..., half:], w[..., :half]], axis=-1)


def _prep_layer(w_in, mla_w_uq, mla_w_ukv, w_out, a_log, dt_bias):
    o = np.cumsum((0, GDN_QKV, GDN_WIDTH, 2 * GDN_HEADS, 2 * GDN_HEADS, Q_LORA, KV_LORA,
                   MLA_ROPE, MLA_WIDTH))
    qkv, za, bl, al, cq, ckv, kpe, zb = (w_in[:, o[n]:o[n + 1]] for n in range(8))
    pad = jnp.zeros((D_MODEL, LANES - 4 * GDN_HEADS), w_in.dtype)
    w_pad = jnp.concatenate([qkv, za, zb, cq, ckv, kpe, _swap_halves(kpe), bl, al, pad],
                            axis=1).astype(BF16)
    wq = mla_w_uq.reshape(Q_LORA, MLA_HEADS, MLA_NOPE + MLA_ROPE)
    wq_ext = jnp.concatenate([wq, _swap_halves(wq[..., MLA_NOPE:])], axis=-1)
    wq_ext = wq_ext.reshape(Q_LORA, MLA_HEADS * QK_PAD).astype(BF16)
    wkv = mla_w_ukv.astype(BF16)
    gpar = jnp.zeros((SUBLANES, LANES), F32)
    n_gate = 2 * GDN_HEADS
    gpar = gpar.at[0, n_gate:2 * n_gate].set(a_log.reshape(-1))
    gpar = gpar.at[1, n_gate:2 * n_gate].set(dt_bias.reshape(-1))
    return w_pad, wq_ext, wkv, w_out.astype(BF16), gpar


def _rope_tables(s):
    pos = jnp.arange(s, dtype=F32)
    inv = ROPE_BASE ** (-jnp.arange(0, MLA_ROPE, 2, dtype=F32) / MLA_ROPE)
    ang = pos[:, None] * inv[None, :]
    cos, sin = jnp.cos(ang), jnp.sin(ang)
    zero = jnp.zeros((s, LANES - MLA_ROPE), F32)
    return (jnp.concatenate([cos, cos, zero], axis=1),
            jnp.concatenate([-sin, sin, zero], axis=1))


def _pick(n, prefs):
    for p in prefs:
        if n % p == 0:
            return p
    return n


def _tiles(b, s):
    t = b * s
    return dict(tm_in=_pick(t, (1024, 512, 256)), tn_in=1024,
                ts_mla=_pick(s, (512, 256)),
                tq=_pick(s, (512, 256, 128)), tk=_pick(s, (1024, 512, 256, 128)),
                tm_out=_pick(t, (512, 256)), gdn_hps=2)


def _layer(x2d, b, s, tl, pre_g, post_g, conv_w, gdn_norm_g, q_norm_g, kv_norm_g,
           w_pad, wq_ext, wkv, w_out, gpar, cos2, sin2):
    proj = _inproj(x2d, pre_g.reshape(1, -1), w_pad, tl["tm_in"], tl["tn_in"])
    prep = _gdn_prep(proj, conv_w, gpar, b, s, tl["gdn_hps"])
    o_f, o_b = _gdn_scan(prep, b, s)
    q, k, v = _mla_proj(proj, q_norm_g.reshape(1, -1), kv_norm_g.reshape(1, -1), wq_ext, wkv,
                        cos2, sin2, b, s, tl["ts_mla"])
    mix_b = _attention(q, k, v, proj, b, s, tl["tq"], tl["tk"])
    return _outproj(o_f.reshape(b * s, -1), o_b.reshape(b * s, -1), proj,
                    mix_b.reshape(b * s, -1), x2d, gdn_norm_g.reshape(1, -1),
                    post_g.reshape(1, -1), w_out, tl["tm_out"])


def _trunk(x, layers, pre_norm_g, post_norm_g, conv_w, gdn_norm_g, mla_q_norm_g, mla_kv_norm_g):
    b, s, d = x.shape
    assert d == D_MODEL and s % TILE == 0
    tl = _tiles(b, s)
    cos2, sin2 = _rope_tables(s)
    x2d = x.reshape(b * s, d)
    for l in range(DEPTH):
        x2d = _layer(x2d, b, s, tl, pre_norm_g[l], post_norm_g[l], conv_w[l], gdn_norm_g[l],
                     mla_q_norm_g[l], mla_kv_norm_g[l], *layers[l], cos2, sin2)
    return x2d.reshape(b, s, d)


def kernel(x_prompt, x_sample, pre_norm_g, post_norm_g, w_in, conv_w, gdn_a_log, gdn_dt_bias,
           gdn_norm_g, mla_q_norm_g, mla_kv_norm_g, mla_w_uq, mla_w_ukv, w_out):
    layers = [_prep_layer(w_in[l], mla_w_uq[l], mla_w_ukv[l], w_out[l], gdn_a_log[l],
                          gdn_dt_bias[l]) for l in range(DEPTH)]
    args = (layers, pre_norm_g, post_norm_g, conv_w, gdn_norm_g, mla_q_norm_g, mla_kv_norm_g)
    return (_trunk(x_prompt, *args), _trunk(x_sample, *args))
```

```python
import functools

import numpy as np
import jax
import jax.numpy as jnp
from jax import lax
from jax.experimental import pallas as pl
from jax.experimental.pallas import tpu as pltpu

F32 = jnp.float32
BF16 = jnp.bfloat16

D_MODEL = 2048
DEPTH = 2
GDN_HEADS = 8
GDN_DK = 128
GDN_DV = 128
GDN_WIDTH = GDN_HEADS * GDN_DV
GDN_QKV = 2 * GDN_HEADS * GDN_DK + GDN_WIDTH
CONV_K = 5
CHUNK = 64
MLA_HEADS = 8
MLA_NOPE = 128
MLA_ROPE = 64
MLA_DV = 128
MLA_WIDTH = MLA_HEADS * MLA_DV
Q_LORA = 512
KV_LORA = 256
ROPE_BASE = 10000.0
D_MIX = GDN_WIDTH + MLA_WIDTH
EPS = 1e-6
MLA_SCALE = (MLA_NOPE + MLA_ROPE) ** -0.5
GDN_SCALE = GDN_DK ** -0.5
LOG2_E = float(np.log2(np.e))

LANES = 128
SUBLANES = 8
VMEM_LIMIT = 56 * 1024 * 1024

COL_QKV = 0
COL_ZA = COL_QKV + GDN_QKV
COL_ZB = COL_ZA + GDN_WIDTH
COL_CQ = COL_ZB + MLA_WIDTH
COL_CKV = COL_CQ + Q_LORA
COL_KPE = COL_CKV + KV_LORA
COL_GATE = COL_KPE + 2 * MLA_ROPE
D_PROJ = COL_GATE + LANES
MISC_W = D_PROJ - COL_CQ
QK_PAD = 2 * LANES
V_PAD = 2 * LANES
QK_SCALE_LOG2 = MLA_SCALE * LOG2_E
TILE = 4 * CHUNK
CHUNK_LOG2 = CHUNK.bit_length() - 1
assert 1 << CHUNK_LOG2 == CHUNK and 2 * CHUNK == LANES

assert D_PROJ == 6144 and MISC_W == 1024 and COL_GATE - COL_CQ == 896


def _cparams(*sem):
    return pltpu.CompilerParams(dimension_semantics=sem, vmem_limit_bytes=VMEM_LIMIT)


def _silu(z):
    return z / (1.0 + jnp.exp(-z))


def _rms(x, g):
    return x * lax.rsqrt(jnp.mean(x * x, axis=-1, keepdims=True) + EPS) * g


def _inproj_kernel(x_ref, g_ref, w_ref, o_ref, h_ref):
    @pl.when(pl.program_id(1) == 0)
    def _():
        h_ref[...] = _rms(x_ref[...], g_ref[...]).astype(BF16)

    o_ref[...] = jnp.dot(h_ref[...], w_ref[...], preferred_element_type=F32)


def _inproj(x2d, g, w_pad, tm, tn):
    t = x2d.shape[0]
    return pl.pallas_call(
        _inproj_kernel,
        out_shape=jax.ShapeDtypeStruct((t, D_PROJ), F32),
        grid=(t // tm, D_PROJ // tn),
        in_specs=[
            pl.BlockSpec((tm, D_MODEL), lambda i, j: (i, 0)),
            pl.BlockSpec((1, D_MODEL), lambda i, j: (0, 0)),
            pl.BlockSpec((D_MODEL, tn), lambda i, j: (0, j)),
        ],
        out_specs=pl.BlockSpec((tm, tn), lambda i, j: (i, j)),
        scratch_shapes=[pltpu.VMEM((tm, D_MODEL), BF16)],
        compiler_params=_cparams("parallel", "arbitrary"),
        name="inproj",
    )(x2d, g, w_pad)


def _mla_proj_kernel(p_ref, gq_ref, gkv_ref, wq_ref, wkv_ref, cos_ref, sin_ref,
                     q_ref, k_ref, v_ref):
    t = p_ref[...]
    cqn = _rms(t[:, 0:Q_LORA], gq_ref[...]).astype(BF16)
    ckn = _rms(t[:, Q_LORA:Q_LORA + KV_LORA], gkv_ref[...]).astype(BF16)
    kpe2 = t[:, COL_KPE - COL_CQ:COL_GATE - COL_CQ]
    qe = jnp.dot(cqn, wq_ref[...], preferred_element_type=F32)
    kve = jnp.dot(ckn, wkv_ref[...], preferred_element_type=F32)
    c2 = cos_ref[...]
    s2 = sin_ref[...]
    krope = (kpe2 * c2 + pltpu.roll(kpe2, MLA_ROPE, axis=1) * s2).astype(BF16)
    ones = jnp.ones((t.shape[0], V_PAD - MLA_DV), BF16)
    for h in range(MLA_HEADS):
        qn = qe[:, h * QK_PAD:h * QK_PAD + LANES]
        qp = qe[:, h * QK_PAD + LANES:(h + 1) * QK_PAD]
        qr = qp * c2 + pltpu.roll(qp, MLA_ROPE, axis=1) * s2
        q_ref[0, h, :, 0:LANES] = (qn * QK_SCALE_LOG2).astype(BF16)
        q_ref[0, h, :, LANES:QK_PAD] = (qr * QK_SCALE_LOG2).astype(BF16)
        k_ref[0, h, :, 0:LANES] = kve[:, h * QK_PAD:h * QK_PAD + LANES].astype(BF16)
        k_ref[0, h, :, LANES:QK_PAD] = krope
        v_ref[0, h, :, 0:MLA_DV] = kve[:, h * QK_PAD + LANES:(h + 1) * QK_PAD].astype(BF16)
        v_ref[0, h, :, MLA_DV:] = ones


def _mla_proj(proj, gq, gkv, wq_ext, wkv, cos2, sin2, b, s, ts):
    nt = s // ts
    hs = MLA_HEADS
    return pl.pallas_call(
        _mla_proj_kernel,
        out_shape=(jax.ShapeDtypeStruct((b, hs, s, QK_PAD), BF16),
                   jax.ShapeDtypeStruct((b, hs, s, QK_PAD), BF16),
                   jax.ShapeDtypeStruct((b, hs, s, V_PAD), BF16)),
        grid=(b, nt),
        in_specs=[
            pl.BlockSpec((ts, MISC_W), lambda bi, i: (bi * nt + i, COL_CQ // MISC_W)),
            pl.BlockSpec((1, Q_LORA), lambda bi, i: (0, 0)),
            pl.BlockSpec((1, KV_LORA), lambda bi, i: (0, 0)),
            pl.BlockSpec((Q_LORA, hs * QK_PAD), lambda bi, i: (0, 0)),
            pl.BlockSpec((KV_LORA, hs * QK_PAD), lambda bi, i: (0, 0)),
            pl.BlockSpec((ts, LANES), lambda bi, i: (i, 0)),
            pl.BlockSpec((ts, LANES), lambda bi, i: (i, 0)),
        ],
        out_specs=(pl.BlockSpec((1, hs, ts, QK_PAD), lambda bi, i: (bi, 0, i, 0)),
                   pl.BlockSpec((1, hs, ts, QK_PAD), lambda bi, i: (bi, 0, i, 0)),
                   pl.BlockSpec((1, hs, ts, V_PAD), lambda bi, i: (bi, 0, i, 0))),
        compiler_params=_cparams("parallel", "parallel"),
        name="mla_proj",
    )(proj, gq, gkv, wq_ext, wkv, cos2, sin2)


def _attn_kernel(q_ref, k_ref, v_ref, z_ref, o_ref, sa_ref, sb_ref, p_ref, m_ref, acc_ref,
                 *, tk, nk, rb):
    q = q_ref[0, 0]
    tq = q.shape[0]
    nt_dims = (((1,), (1,)), ((), ()))

    def scores(j, s_ref):
        off = pl.multiple_of(j * tk, tk)
        s_ref[...] = lax.dot_general(q, k_ref[0, 0, pl.ds(off, tk), :], nt_dims,
                                     preferred_element_type=F32)

    def softmax_and_pv(j, s_ref):
        blocks = [slice(r * rb, (r + 1) * rb) for r in range(tq // rb)]
        lane_tiles = [slice(t * LANES, (t + 1) * LANES) for t in range(tk // LANES)]
        mx_parts = []
        for rows in blocks:
            mx = s_ref[rows, lane_tiles[0]]
            for lt in lane_tiles[1:]:
                mx = jnp.maximum(mx, s_ref[rows, lt])
            mx_parts.append(mx)
        mx_all = jnp.concatenate(mx_parts, axis=0)
        m_old = m_ref[...]
        row_max = jnp.broadcast_to(jnp.max(mx_all, axis=-1, keepdims=True), mx_all.shape)
        m_new = jnp.maximum(m_old, row_max)
        alpha = jnp.exp2(m_old - m_new)
        m_ref[...] = m_new
        for rows in blocks:
            m_b = m_new[rows]
            for lt in lane_tiles:
                p_ref[rows, lt] = jnp.exp2(s_ref[rows, lt] - m_b).astype(BF16)
        off = pl.multiple_of(j * tk, tk)
        pv = jnp.dot(p_ref[...], v_ref[0, 0, pl.ds(off, tk), :], preferred_element_type=F32)
        acc_ref[...] = acc_ref[...] * jnp.concatenate([alpha, alpha], axis=1) + pv

    m_ref[...] = jnp.full(m_ref.shape, -jnp.inf, F32)
    acc_ref[...] = jnp.zeros(acc_ref.shape, F32)
    scores(0, sa_ref)

    def body(jj, carry):
        j = 2 * jj
        scores(j + 1, sb_ref)
        softmax_and_pv(j, sa_ref)
        scores(j + 2, sa_ref)
        softmax_and_pv(j + 1, sb_ref)
        return carry

    n_pairs = (nk - 1) // 2
    lax.fori_loop(0, n_pairs, body, 0)
    j0 = 2 * n_pairs
    if nk - j0 == 2:
        scores(j0 + 1, sb_ref)
        softmax_and_pv(j0, sa_ref)
        softmax_and_pv(j0 + 1, sb_ref)
    else:
        softmax_and_pv(j0, sa_ref)
    acc = acc_ref[...]
    o_ref[0] = ((acc[:, 0:MLA_DV] / acc[:, MLA_DV:]) * _silu(z_ref[...])).astype(BF16)


def _attention(q, k, v, proj, b, s, tq, tk):
    hs = MLA_HEADS
    nq = s // tq
    zb0 = COL_ZB // LANES
    return pl.pallas_call(
        functools.partial(_attn_kernel, tk=tk, nk=s // tk, rb=2 * SUBLANES),
        out_shape=jax.ShapeDtypeStruct((b, s, MLA_WIDTH), BF16),
        grid=(b, hs, nq),
        in_specs=[
            pl.BlockSpec((1, 1, tq, QK_PAD), lambda bi, h, i: (bi, h, i, 0)),
            pl.BlockSpec((1, 1, s, QK_PAD), lambda bi, h, i: (bi, h, 0, 0)),
            pl.BlockSpec((1, 1, s, V_PAD), lambda bi, h, i: (bi, h, 0, 0)),
            pl.BlockSpec((tq, LANES), lambda bi, h, i: (bi * nq + i, zb0 + h)),
        ],
        out_specs=pl.BlockSpec((1, tq, MLA_DV), lambda bi, h, i: (bi, i, h)),
        scratch_shapes=[pltpu.VMEM((tq, tk), F32), pltpu.VMEM((tq, tk), F32),
                        pltpu.VMEM((tq, tk), BF16),
                        pltpu.VMEM((tq, LANES), F32), pltpu.VMEM((tq, V_PAD), F32)],
        compiler_params=_cparams("parallel", "parallel", "arbitrary"),
        name="attention",
    )(q, k, v, proj)


def _tile_masks():
    r = lax.broadcasted_iota(jnp.int32, (TILE, TILE), 0)
    c = lax.broadcasted_iota(jnp.int32, (TILE, TILE), 1)
    same = (r >> CHUNK_LOG2) == (c >> CHUNK_LOG2)
    return (same & (r >= c)).astype(F32), (same & (r <= c)).astype(F32)


def _col(x, lane):
    li = lax.broadcasted_iota(jnp.int32, x.shape, 1)
    col = jnp.sum(jnp.where(li == lane, x, 0.0), axis=1, keepdims=True)
    return jnp.broadcast_to(col, x.shape)


def _dup_chunks_t(cs):
    parts = []
    for c in range(TILE // CHUNK):
        blk = cs[c * CHUNK:(c + 1) * CHUNK, :]
        parts += [blk, blk]
    return jnp.concatenate(parts, axis=0).T


def _gdn_prep_kernel(q_ref, qp_ref, qn_ref, k_ref, kp_ref, kn_ref, v_ref, vp_ref, vn_ref,
                     cwq_ref, cwk_ref, cwv_ref, gate_ref, gpar_ref,
                     uf_ref, wf_ref, qdf_ref, kdf_ref, atf_ref,
                     ub_ref, wb_ref, qdb_ref, kdb_ref, atb_ref, egl_ref,
                     xs_ref, g_ref, csl_ref, csu_ref, cslt_ref, csut_ref):
    i = pl.program_id(1)
    nt = pl.num_programs(1)
    hg = pl.program_id(2)
    hps = q_ref.shape[1] // LANES

    @pl.when(hg == 0)
    def _():
        t = gate_ref[...]
        a_log = gpar_ref[0:1, :]
        dt_b = gpar_ref[1:2, :]
        li = lax.broadcasted_iota(jnp.int32, t.shape, 1)
        xg = t + dt_b
        sp = jnp.maximum(xg, 0.0) + jnp.log1p(jnp.exp(-jnp.abs(xg)))
        gdec = -jnp.exp(a_log) * sp
        beta = 1.0 / (1.0 + jnp.exp(-t))
        n_gate = 2 * GDN_HEADS
        g = jnp.where(li < n_gate, beta, jnp.where(li < 2 * n_gate, gdec, 0.0))
        g_ref[...] = g
        tri_l, tri_u = _tile_masks()
        hi = g.astype(BF16)
        r1 = g - hi.astype(F32)
        mid = r1.astype(BF16)
        lo = (r1 - mid.astype(F32)).astype(BF16)
        pieces = jnp.concatenate([hi, mid, lo], axis=1)

        def cumsum(tri):
            c3 = jnp.dot(tri.astype(BF16), pieces, preferred_element_type=F32)
            return c3[:, 0:LANES] + c3[:, LANES:2 * LANES] + c3[:, 2 * LANES:]

        csl = cumsum(tri_l)
        csu = cumsum(tri_u)
        csl_ref[...] = csl
        csu_ref[...] = csu
        cslt_ref[...] = _dup_chunks_t(csl)
        csut_ref[...] = _dup_chunks_t(csu)

    def conv_silu(main_ref, prev_ref, next_ref, w_ref):
        xs_ref[0:SUBLANES, :] = jnp.where(i > 0, prev_ref[...], 0.0)
        xs_ref[SUBLANES:SUBLANES + TILE, :] = main_ref[...]
        xs_ref[SUBLANES + TILE:, :] = jnp.where(i < nt - 1, next_ref[...], 0.0)
        pad = (CONV_K - 1) // 2
        xs = xs_ref[...]
        n = xs.shape[0]
        acc = None
        for tap in range(CONV_K):
            sh = xs if tap == pad else pltpu.roll(xs, (pad - tap) % n, axis=0)
            term = sh[SUBLANES:SUBLANES + TILE, :] * w_ref[tap:tap + 1, :]
            acc = term if acc is None else acc + term
        return _silu(acc)

    def l2n(x):
        return x * lax.rsqrt(jnp.sum(x * x, axis=-1, keepdims=True) + EPS)

    q_all = conv_silu(q_ref, qp_ref, qn_ref, cwq_ref)
    k_all = conv_silu(k_ref, kp_ref, kn_ref, cwk_ref)
    v_all = conv_silu(v_ref, vp_ref, vn_ref, cwv_ref)

    ri = lax.broadcasted_iota(jnp.int32, (CHUNK, LANES), 0)
    li = lax.broadcasted_iota(jnp.int32, (CHUNK, LANES), 1)
    lj = li & (CHUNK - 1)
    left = li < CHUNK
    left_f = left.astype(F32)
    eye_left = (ri == li).astype(F32)
    zeros_w = jnp.zeros((CHUNK, LANES), BF16)
    zeros_r = jnp.zeros((CHUNK, 2 * LANES), BF16)
    nt_dims = (((1,), (1,)), ((), ()))
    nchunk = TILE // CHUNK

    g = g_ref[...]
    csl = csl_ref[...]
    csu = csu_ref[...]
    nh = GDN_HEADS
    lower = (ri >= lj).astype(F32)
    slower = (ri > lj).astype(F32)
    upper = (ri <= lj).astype(F32)
    supper = (ri < lj).astype(F32)

    def setup_head(hh):
        h = hg * hps + hh
        cols = slice(hh * LANES, (hh + 1) * LANES)
        q = l2n(q_all[:, cols]) * GDN_SCALE
        k = l2n(k_all[:, cols])
        v = v_all[:, cols]
        qb = q.astype(BF16)
        kb16 = k.astype(BF16)
        grams = []
        for c in range(nchunk):
            rows = slice(c * CHUNK, (c + 1) * CHUNK)
            k2 = jnp.concatenate([kb16[rows], kb16[rows]], axis=0)
            grams.append((lax.dot_general(kb16[rows], k2, nt_dims, preferred_element_type=F32),
                          lax.dot_general(qb[rows], k2, nt_dims, preferred_element_type=F32)))

        def setup_direction(beta_lane, g_lane, cs, cs_other, cst_ref, tri, stri,
                            u_ref, w_ref, qd_ref, kd_ref, at_ref, egl_row):
            beta = _col(g, beta_lane)
            gc = _col(cs, g_lane)
            rest = _col(cs_other - g, g_lane)
            egc = jnp.exp(gc)
            qd_ref[0, :, cols] = (q * egc).astype(BF16)
            kd_ref[0, :, cols] = (k * jnp.exp(rest)).astype(BF16)
            vb = (v * beta).astype(BF16)
            kbg = (k * (beta * egc)).astype(BF16)
            tot = jnp.exp(gc + rest)
            gc_rows = cst_ref[pl.ds(g_lane, 1), :]
            chains = []
            for c in range(nchunk):
                rows = slice(c * CHUNK, (c + 1) * CHUNK)
                kk2, qk2 = grams[c]
                gc_row = gc_rows[:, c * LANES:(c + 1) * LANES]
                decay = jnp.exp(jnp.minimum(gc[rows] - gc_row, 0.0))
                x = jnp.where(left, eye_left, -(beta[rows] * kk2 * decay * stri))
                half = left_f if c % 2 == 0 else 1.0 - left_f
                at_ref[0, rows, cols] = (qk2 * decay * (tri * half)).astype(BF16)
                egl_ref[0, c, pl.ds(egl_row, 1), :] = tot[c * CHUNK:c * CHUNK + 1, :]
                rhs = jnp.concatenate(
                    [jnp.concatenate([vb[rows], kbg[rows]], axis=1), zeros_r], axis=0)
                chains.append((x, rhs, rows, cols, u_ref, w_ref))
            return chains

        return (setup_direction(h, 2 * nh + h, csl, csu, cslt_ref, lower, slower,
                                uf_ref, wf_ref, qdf_ref, kdf_ref, atf_ref, h)
                + setup_direction(nh + h, 3 * nh + h, csu, csl, csut_ref, upper, supper,
                                  ub_ref, wb_ref, qdb_ref, kdb_ref, atb_ref, nh + h))

    chains = []
    for hh in range(hps):
        chains += setup_head(hh)
    xs = [ch[0] for ch in chains]
    for _ in range(CHUNK_LOG2):
        nxt = []
        for x in xs:
            xb = x.astype(BF16)
            nxt.append(x * left_f + jnp.dot(xb, jnp.concatenate([zeros_w, xb], axis=0),
                                            preferred_element_type=F32))
        xs = nxt
    for x, (_, rhs, rows, cols, u_ref, w_ref) in zip(xs, chains):
        uw = jnp.dot(x.astype(BF16), rhs, preferred_element_type=F32)
        u_ref[0, rows, cols] = uw[:, 0:GDN_DV].astype(BF16)
        w_ref[0, rows, cols] = uw[:, GDN_DV:].astype(BF16)


def _gdn_prep(proj, conv_w, gpar, b, s, hps):
    nt = s // TILE
    hs = GDN_HEADS
    rows8 = TILE // SUBLANES
    last8 = b * s // SUBLANES - 1
    wid = hps * LANES
    ng = hs // hps

    def main(col0):
        return pl.BlockSpec((TILE, wid), lambda bi, i, h: (bi * nt + i, col0 + h))

    def prev(col0):
        return pl.BlockSpec((SUBLANES, wid),
                            lambda bi, i, h: (jnp.maximum((bi * nt + i) * rows8 - 1, 0), col0 + h))

    def nxt(col0):
        return pl.BlockSpec((SUBLANES, wid),
                            lambda bi, i, h: (jnp.minimum((bi * nt + i + 1) * rows8, last8), col0 + h))

    def cw(col0):
        return pl.BlockSpec((CONV_K, wid), lambda bi, i, h: (0, col0 + h))

    qc, kc, vc = 0, ng, 2 * ng
    seq = jax.ShapeDtypeStruct((b, s, GDN_WIDTH), BF16)
    seq_spec = pl.BlockSpec((1, TILE, wid), lambda bi, i, h: (bi, i, h))
    nchunk = TILE // CHUNK
    egl = jax.ShapeDtypeStruct((b * nt, nchunk, 2 * hs, LANES), F32)
    egl_spec = pl.BlockSpec((1, nchunk, 2 * hs, LANES), lambda bi, i, h: (bi * nt + i, 0, 0, 0))
    return pl.pallas_call(
        _gdn_prep_kernel,
        out_shape=(seq,) * 10 + (egl,),
        grid=(b, nt, ng),
        in_specs=[main(qc), prev(qc), nxt(qc), main(kc), prev(kc), nxt(kc),
                  main(vc), prev(vc), nxt(vc), cw(qc), cw(kc), cw(vc),
                  pl.BlockSpec((TILE, LANES), lambda bi, i, h: (bi * nt + i, COL_GATE // LANES)),
                  pl.BlockSpec((SUBLANES, LANES), lambda bi, i, h: (0, 0))],
        out_specs=(seq_spec,) * 10 + (egl_spec,),
        scratch_shapes=[pltpu.VMEM((TILE + 2 * SUBLANES, wid), F32),
                        pltpu.VMEM((TILE, LANES), F32), pltpu.VMEM((TILE, LANES), F32),
                        pltpu.VMEM((TILE, LANES), F32), pltpu.VMEM((LANES, 2 * TILE), F32),
                        pltpu.VMEM((LANES, 2 * TILE), F32)],
        compiler_params=_cparams("parallel", "parallel", "arbitrary"),
        name="gdn_prep",
    )(*([proj] * 9), conv_w, conv_w, conv_w, proj, gpar)


def _gdn_scan_kernel(uf_ref, wf_ref, qdf_ref, kdf_ref, atf_ref, eglf_ref,
                     ub_ref, wb_ref, qdb_ref, kdb_ref, atb_ref, eglb_ref,
                     of_ref, ob_ref, st_ref):
    @pl.when(pl.program_id(1) == 0)
    def _():
        st_ref[...] = jnp.zeros_like(st_ref)

    nchunk = TILE // CHUNK
    tn_dims = (((0,), (0,)), ((), ()))

    fwd_refs = (uf_ref, wf_ref, qdf_ref, kdf_ref, atf_ref, eglf_ref, of_ref)
    bwd_refs = (ub_ref, wb_ref, qdb_ref, kdb_ref, atb_ref, eglb_ref, ob_ref)

    def step(c, carry):
        chains = []
        for hd in range(GDN_HEADS):
            chains.append((hd, c, hd, fwd_refs))
            chains.append((GDN_HEADS + hd, nchunk - 1 - c, hd, bwd_refs))
        stage1 = []
        for idx, cc, hd, (u_ref, w_ref, qd_ref, kd_ref, at_ref, egl_ref, o_ref) in chains:
            rows = pl.ds(pl.multiple_of(cc * CHUNK, CHUNK), CHUNK)
            cols = slice(hd * LANES, (hd + 1) * LANES)
            sb = st_ref[idx].astype(BF16)
            ws = jnp.dot(w_ref[0, rows, cols], sb, preferred_element_type=F32)
            qs = jnp.dot(qd_ref[0, rows, cols], sb, preferred_element_type=F32)
            stage1.append((rows, cols, ws, qs))
        for (idx, cc, hd, refs), (rows, cols, ws, qs) in zip(chains, stage1):
            u_ref, w_ref, qd_ref, kd_ref, at_ref, egl_ref, o_ref = refs
            v_new = (u_ref[0, rows, cols].astype(F32) - ws).astype(BF16)
            v_pair = jnp.concatenate([v_new, v_new], axis=0)
            o_ref[0, rows, cols] = qs + jnp.dot(at_ref[0, rows, cols], v_pair,
                                                preferred_element_type=F32)
            decay = egl_ref[0, pl.ds(cc, 1), idx, :]
            upd = lax.dot_general(kd_ref[0, rows, cols], v_new, tn_dims,
                                  preferred_element_type=F32)
            st_ref[idx] = st_ref[idx] * decay + upd
        return carry

    lax.fori_loop(0, nchunk, step, 0)


def _gdn_scan(prep, b, s):
    uf, wf, qdf, kdf, atf, ub, wb, qdb, kdb, atb, egl = prep
    nt = s // TILE
    hs = GDN_HEADS
    nchunk = TILE // CHUNK
    fwd = pl.BlockSpec((1, TILE, GDN_WIDTH), lambda bi, i: (bi, i, 0))
    bwd = pl.BlockSpec((1, TILE, GDN_WIDTH), lambda bi, i: (bi, nt - 1 - i, 0))
    egl_f = pl.BlockSpec((1, nchunk, 2 * hs, LANES), lambda bi, i: (bi * nt + i, 0, 0, 0))
    egl_b = pl.BlockSpec((1, nchunk, 2 * hs, LANES), lambda bi, i: (bi * nt + nt - 1 - i, 0, 0, 0))
    out = jax.ShapeDtypeStruct((b, s, GDN_WIDTH), F32)
    return pl.pallas_call(
        _gdn_scan_kernel,
        out_shape=(out, out),
        grid=(b, nt),
        in_specs=[fwd] * 5 + [egl_f] + [bwd] * 5 + [egl_b],
        out_specs=(fwd, bwd),
        scratch_shapes=[pltpu.VMEM((2 * hs, GDN_DK, GDN_DV), F32)],
        compiler_params=_cparams("parallel", "arbitrary"),
        name="gdn_scan",
    )(uf, wf, qdf, kdf, atf, egl, ub, wb, qdb, kdb, atb, egl)


def _outproj_kernel(of_ref, ob_ref, za_ref, mixb_ref, x_ref, gn_ref, gp_ref, w_ref, y_ref):
    o = of_ref[...] + ob_ref[...]
    za = za_ref[...]
    parts = []
    for h in range(GDN_HEADS):
        sl = slice(h * GDN_DV, (h + 1) * GDN_DV)
        parts.append((_rms(o[:, sl], gn_ref[...]) * _silu(za[:, sl])).astype(BF16))
    mix_a = jnp.concatenate(parts, axis=1)
    y = jnp.dot(mix_a, w_ref[0:GDN_WIDTH, :], preferred_element_type=F32)
    y = y + jnp.dot(mixb_ref[...], w_ref[GDN_WIDTH:, :], preferred_element_type=F32)
    y_ref[...] = x_ref[...] + _rms(y, gp_ref[...])


def _outproj(o_f, o_b, proj, mix_b, x2d, gn, gp, w_out, tm):
    t = x2d.shape[0]
    row = lambda i: (i, 0)
    const = lambda i: (0, 0)
    return pl.pallas_call(
        _outproj_kernel,
        out_shape=jax.ShapeDtypeStruct((t, D_MODEL), F32),
        grid=(t // tm,),
        in_specs=[
            pl.BlockSpec((tm, GDN_WIDTH), row),
            pl.BlockSpec((tm, GDN_WIDTH), row),
            pl.BlockSpec((tm, GDN_WIDTH), lambda i: (i, COL_ZA // GDN_WIDTH)),
            pl.BlockSpec((tm, MLA_WIDTH), row),
            pl.BlockSpec((tm, D_MODEL), row),
            pl.BlockSpec((1, GDN_DV), const),
            pl.BlockSpec((1, D_MODEL), const),
            pl.BlockSpec((D_MIX, D_MODEL), const),
        ],
        out_specs=pl.BlockSpec((tm, D_MODEL), row),
        compiler_params=_cparams("parallel"),
        name="outproj",
    )(o_f, o_b, proj, mix_b, x2d, gn, gp, w_out)


def _swap_halves(w):
    half = MLA_ROPE // 2
    return jnp.concatenate([w[..., half:], w[..., :half]], axis=-1)


def _prep_layer(w_in, mla_w_uq, mla_w_ukv, w_out, a_log, dt_bias):
    o = np.cumsum((0, GDN_QKV, GDN_WIDTH, 2 * GDN_HEADS, 2 * GDN_HEADS, Q_LORA, KV_LORA,
                   MLA_ROPE, MLA_WIDTH))
    qkv, za, bl, al, cq, ckv, kpe, zb = (w_in[:, o[n]:o[n + 1]] for n in range(8))
    pad = jnp.zeros((D_MODEL, LANES - 4 * GDN_HEADS), w_in.dtype)
    w_pad = jnp.concatenate([qkv, za, zb, cq, ckv, kpe, _swap_halves(kpe), bl, al, pad],
                            axis=1).astype(BF16)
    wq = mla_w_uq.reshape(Q_LORA, MLA_HEADS, MLA_NOPE + MLA_ROPE)
    wq_ext = jnp.concatenate([wq, _swap_halves(wq[..., MLA_NOPE:])], axis=-1)
    wq_ext = wq_ext.reshape(Q_LORA, MLA_HEADS * QK_PAD).astype(BF16)
    wkv = mla_w_ukv.astype(BF16)
    gpar = jnp.zeros((SUBLANES, LANES), F32)
    n_gate = 2 * GDN_HEADS
    gpar = gpar.at[0, n_gate:2 * n_gate].set(a_log.reshape(-1))
    gpar = gpar.at[1, n_gate:2 * n_gate].set(dt_bias.reshape(-1))
    return w_pad, wq_ext, wkv, w_out.astype(BF16), gpar


def _rope_tables(s):
    pos = jnp.arange(s, dtype=F32)
    inv = ROPE_BASE ** (-jnp.arange(0, MLA_ROPE, 2, dtype=F32) / MLA_ROPE)
    ang = pos[:, None] * inv[None, :]
    cos, sin = jnp.cos(ang), jnp.sin(ang)
    zero = jnp.zeros((s, LANES - MLA_ROPE), F32)
    return (jnp.concatenate([cos, cos, zero], axis=1),
            jnp.concatenate([-sin, sin, zero], axis=1))


def _pick(n, prefs):
    for p in prefs:
        if n % p == 0:
            return p
    return n


def _tiles(b, s):
    t = b * s
    return dict(tm_in=_pick(t, (1024, 512, 256)), tn_in=1024,
                ts_mla=_pick(s, (512, 256)),
                tq=_pick(s, (512, 256, 128)), tk=_pick(s, (1024, 512, 256, 128)),
                tm_out=_pick(t, (512, 256)), gdn_hps=4)


def _layer(x2d, b, s, tl, pre_g, post_g, conv_w, gdn_norm_g, q_norm_g, kv_norm_g,
           w_pad, wq_ext, wkv, w_out, gpar, cos2, sin2):
    proj = _inproj(x2d, pre_g.reshape(1, -1), w_pad, tl["tm_in"], tl["tn_in"])
    prep = _gdn_prep(proj, conv_w, gpar, b, s, tl["gdn_hps"])
    o_f, o_b = _gdn_scan(prep, b, s)
    q, k, v = _mla_proj(proj, q_norm_g.reshape(1, -1), kv_norm_g.reshape(1, -1), wq_ext, wkv,
                        cos2, sin2, b, s, tl["ts_mla"])
    mix_b = _attention(q, k, v, proj, b, s, tl["tq"], tl["tk"])
    return _outproj(o_f.reshape(b * s, -1), o_b.reshape(b * s, -1), proj,
                    mix_b.reshape(b * s, -1), x2d, gdn_norm_g.reshape(1, -1),
                    post_g.reshape(1, -1), w_out, tl["tm_out"])


def _trunk(x, layers, pre_norm_g, post_norm_g, conv_w, gdn_norm_g, mla_q_norm_g, mla_kv_norm_g):
    b, s, d = x.shape
    assert d == D_MODEL and s % TILE == 0
    tl = _tiles(b, s)
    cos2, sin2 = _rope_tables(s)
    x2d = x.reshape(b * s, d)
    for l in range(DEPTH):
        x2d = _layer(x2d, b, s, tl, pre_norm_g[l], post_norm_g[l], conv_w[l], gdn_norm_g[l],
                     mla_q_norm_g[l], mla_kv_norm_g[l], *layers[l], cos2, sin2)
    return x2d.reshape(b, s, d)


def kernel(x_prompt, x_sample, pre_norm_g, post_norm_g, w_in, conv_w, gdn_a_log, gdn_dt_bias,
           gdn_norm_g, mla_q_norm_g, mla_kv_norm_g, mla_w_uq, mla_w_ukv, w_out):
    layers = [_prep_layer(w_in[l], mla_w_uq[l], mla_w_ukv[l], w_out[l], gdn_a_log[l],
                          gdn_dt_bias[l]) for l in range(DEPTH)]
    args = (layers, pre_norm_g, post_norm_g, conv_w, gdn_norm_g, mla_q_norm_g, mla_kv_norm_g)
    return (_trunk(x_prompt, *args), _trunk(x_sample, *args))
```

```python
import functools

import numpy as np
import jax
import jax.numpy as jnp
from jax import lax
from jax.experimental import pallas as pl
from jax.experimental.pallas import tpu as pltpu

F32 = jnp.float32
BF16 = jnp.bfloat16

D_MODEL = 2048
DEPTH = 2
GDN_HEADS = 8
GDN_DK = 128
GDN_DV = 128
GDN_WIDTH = GDN_HEADS * GDN_DV
GDN_QKV = 2 * GDN_HEADS * GDN_DK + GDN_WIDTH
CONV_K = 5
CHUNK = 64
MLA_HEADS = 8
MLA_NOPE = 128
MLA_ROPE = 64
MLA_DV = 128
MLA_WIDTH = MLA_HEADS * MLA_DV
Q_LORA = 512
KV_LORA = 256
ROPE_BASE = 10000.0
D_MIX = GDN_WIDTH + MLA_WIDTH
EPS = 1e-6
MLA_SCALE = (MLA_NOPE + MLA_ROPE) ** -0.5
GDN_SCALE = GDN_DK ** -0.5
LOG2_E = float(np.log2(np.e))

LANES = 128
SUBLANES = 8
VMEM_LIMIT = 56 * 1024 * 1024

COL_QKV = 0
COL_ZA = COL_QKV + GDN_QKV
COL_ZB = COL_ZA + GDN_WIDTH
COL_CQ = COL_ZB + MLA_WIDTH
COL_CKV = COL_CQ + Q_LORA
COL_KPE = COL_CKV + KV_LORA
COL_GATE = COL_KPE + 2 * MLA_ROPE
D_PROJ = COL_GATE + LANES
MISC_W = D_PROJ - COL_CQ
QK_PAD = 2 * LANES
V_PAD = 2 * LANES
QK_SCALE_LOG2 = MLA_SCALE * LOG2_E
TILE = 4 * CHUNK
CHUNK_LOG2 = CHUNK.bit_length() - 1
assert 1 << CHUNK_LOG2 == CHUNK and 2 * CHUNK == LANES

assert D_PROJ == 6144 and MISC_W == 1024 and COL_GATE - COL_CQ == 896


def _cparams(*sem):
    return pltpu.CompilerParams(dimension_semantics=sem, vmem_limit_bytes=VMEM_LIMIT)


def _silu(z):
    return z / (1.0 + jnp.exp(-z))


def _rms(x, g):
    return x * lax.rsqrt(jnp.mean(x * x, axis=-1, keepdims=True) + EPS) * g


def _inproj_kernel(x_ref, g_ref, w_ref, o_ref, h_ref):
    @pl.when(pl.program_id(1) == 0)
    def _():
        h_ref[...] = _rms(x_ref[...], g_ref[...]).astype(BF16)

    o_ref[...] = jnp.dot(h_ref[...], w_ref[...], preferred_element_type=F32)


def _inproj(x2d, g, w_pad, tm, tn):
    t = x2d.shape[0]
    return pl.pallas_call(
        _inproj_kernel,
        out_shape=jax.ShapeDtypeStruct((t, D_PROJ), F32),
        grid=(t // tm, D_PROJ // tn),
        in_specs=[
            pl.BlockSpec((tm, D_MODEL), lambda i, j: (i, 0)),
            pl.BlockSpec((1, D_MODEL), lambda i, j: (0, 0)),
            pl.BlockSpec((D_MODEL, tn), lambda i, j: (0, j)),
        ],
        out_specs=pl.BlockSpec((tm, tn), lambda i, j: (i, j)),
        scratch_shapes=[pltpu.VMEM((tm, D_MODEL), BF16)],
        compiler_params=_cparams("parallel", "arbitrary"),
        name="inproj",
    )(x2d, g, w_pad)


def _mla_proj_kernel(p_ref, gq_ref, gkv_ref, wq_ref, wkv_ref, cos_ref, sin_ref,
                     q_ref, k_ref, v_ref):
    t = p_ref[...]
    cqn = _rms(t[:, 0:Q_LORA], gq_ref[...]).astype(BF16)
    ckn = _rms(t[:, Q_LORA:Q_LORA + KV_LORA], gkv_ref[...]).astype(BF16)
    kpe2 = t[:, COL_KPE - COL_CQ:COL_GATE - COL_CQ]
    qe = jnp.dot(cqn, wq_ref[...], preferred_element_type=F32)
    kve = jnp.dot(ckn, wkv_ref[...], preferred_element_type=F32)
    c2 = cos_ref[...]
    s2 = sin_ref[...]
    krope = (kpe2 * c2 + pltpu.roll(kpe2, MLA_ROPE, axis=1) * s2).astype(BF16)
    ones = jnp.ones((t.shape[0], V_PAD - MLA_DV), BF16)
    for h in range(MLA_HEADS):
        qn = qe[:, h * QK_PAD:h * QK_PAD + LANES]
        qp = qe[:, h * QK_PAD + LANES:(h + 1) * QK_PAD]
        qr = qp * c2 + pltpu.roll(qp, MLA_ROPE, axis=1) * s2
        q_ref[0, h, :, 0:LANES] = (qn * QK_SCALE_LOG2).astype(BF16)
        q_ref[0, h, :, LANES:QK_PAD] = (qr * QK_SCALE_LOG2).astype(BF16)
        k_ref[0, h, :, 0:LANES] = kve[:, h * QK_PAD:h * QK_PAD + LANES].astype(BF16)
        k_ref[0, h, :, LANES:QK_PAD] = krope
        v_ref[0, h, :, 0:MLA_DV] = kve[:, h * QK_PAD + LANES:(h + 1) * QK_PAD].astype(BF16)
        v_ref[0, h, :, MLA_DV:] = ones


def _mla_proj(proj, gq, gkv, wq_ext, wkv, cos2, sin2, b, s, ts):
    nt = s // ts
    hs = MLA_HEADS
    return pl.pallas_call(
        _mla_proj_kernel,
        out_shape=(jax.ShapeDtypeStruct((b, hs, s, QK_PAD), BF16),
                   jax.ShapeDtypeStruct((b, hs, s, QK_PAD), BF16),
                   jax.ShapeDtypeStruct((b, hs, s, V_PAD), BF16)),
        grid=(b, nt),
        in_specs=[
            pl.BlockSpec((ts, MISC_W), lambda bi, i: (bi * nt + i, COL_CQ // MISC_W)),
            pl.BlockSpec((1, Q_LORA), lambda bi, i: (0, 0)),
            pl.BlockSpec((1, KV_LORA), lambda bi, i: (0, 0)),
            pl.BlockSpec((Q_LORA, hs * QK_PAD), lambda bi, i: (0, 0)),
            pl.BlockSpec((KV_LORA, hs * QK_PAD), lambda bi, i: (0, 0)),
            pl.BlockSpec((ts, LANES), lambda bi, i: (i, 0)),
            pl.BlockSpec((ts, LANES), lambda bi, i: (i, 0)),
        ],
        out_specs=(pl.BlockSpec((1, hs, ts, QK_PAD), lambda bi, i: (bi, 0, i, 0)),
                   pl.BlockSpec((1, hs, ts, QK_PAD), lambda bi, i: (bi, 0, i, 0)),
                   pl.BlockSpec((1, hs, ts, V_PAD), lambda bi, i: (bi, 0, i, 0))),
        compiler_params=_cparams("parallel", "parallel"),
        name="mla_proj",
    )(proj, gq, gkv, wq_ext, wkv, cos2, sin2)


def _attn_kernel(q_ref, k_ref, v_ref, z_ref, o_ref, sa_ref, sb_ref, p_ref, m_ref, acc_ref,
                 *, tq, tk, rb):
    s_len = k_ref.shape[2]
    nq, nk = s_len // tq, s_len // tk
    nt_dims = (((1,), (1,)), ((), ()))
    blocks = [slice(r * rb, (r + 1) * rb) for r in range(tq // rb)]
    lane_tiles = [slice(t * LANES, (t + 1) * LANES) for t in range(tk // LANES)]

    def scores(i, j, s_ref):
        q = q_ref[0, 0, pl.ds(pl.multiple_of(i * tq, tq), tq), :]
        s_ref[...] = lax.dot_general(q, k_ref[0, 0, j * tk:(j + 1) * tk, :], nt_dims,
                                     preferred_element_type=F32)

    def softmax_and_pv(j, s_ref):
        mx_parts = []
        for rows in blocks:
            mx = s_ref[rows, lane_tiles[0]]
            for lt in lane_tiles[1:]:
                mx = jnp.maximum(mx, s_ref[rows, lt])
            mx_parts.append(mx)
        mx_all = jnp.concatenate(mx_parts, axis=0)
        row_max = jnp.broadcast_to(jnp.max(mx_all, axis=-1, keepdims=True), mx_all.shape)
        if j == 0:
            m_new = row_max
        else:
            m_old = m_ref[...]
            m_new = jnp.maximum(m_old, row_max)
            alpha = jnp.exp2(m_old - m_new)
        m_ref[...] = m_new
        for rows in blocks:
            m_b = m_new[rows]
            for lt in lane_tiles:
                p_ref[rows, lt] = jnp.exp2(s_ref[rows, lt] - m_b).astype(BF16)
        pv = jnp.dot(p_ref[...], v_ref[0, 0, j * tk:(j + 1) * tk, :],
                     preferred_element_type=F32)
        if j == 0:
            acc_ref[...] = pv
        else:
            acc_ref[...] = acc_ref[...] * jnp.concatenate([alpha, alpha], axis=1) + pv

    bufs = (sa_ref, sb_ref)
    scores(0, 0, sa_ref)

    def body(i, carry):
        for j in range(nk):
            if j + 1 < nk:
                scores(i, j + 1, bufs[(j + 1) % 2])
            else:
                scores(jnp.minimum(i + 1, nq - 1), 0, bufs[0])
            softmax_and_pv(j, bufs[j % 2])
        rows = pl.ds(pl.multiple_of(i * tq, tq), tq)
        acc = acc_ref[...]
        o_ref[0, rows, :] = ((acc[:, 0:MLA_DV] / acc[:, MLA_DV:])
                             * _silu(z_ref[rows, :])).astype(BF16)
        return carry

    lax.fori_loop(0, nq, body, 0)


def _attention(q, k, v, proj, b, s, tq, tk):
    hs = MLA_HEADS
    zb0 = COL_ZB // LANES
    assert (s // tk) % 2 == 0
    return pl.pallas_call(
        functools.partial(_attn_kernel, tq=tq, tk=tk, rb=2 * SUBLANES),
        out_shape=jax.ShapeDtypeStruct((b, s, MLA_WIDTH), BF16),
        grid=(b, hs),
        in_specs=[
            pl.BlockSpec((1, 1, s, QK_PAD), lambda bi, h: (bi, h, 0, 0)),
            pl.BlockSpec((1, 1, s, QK_PAD), lambda bi, h: (bi, h, 0, 0)),
            pl.BlockSpec((1, 1, s, V_PAD), lambda bi, h: (bi, h, 0, 0)),
            pl.BlockSpec((s, LANES), lambda bi, h: (bi, zb0 + h)),
        ],
        out_specs=pl.BlockSpec((1, s, MLA_DV), lambda bi, h: (bi, 0, h)),
        scratch_shapes=[pltpu.VMEM((tq, tk), F32), pltpu.VMEM((tq, tk), F32),
                        pltpu.VMEM((tq, tk), BF16),
                        pltpu.VMEM((tq, LANES), F32), pltpu.VMEM((tq, V_PAD), F32)],
        compiler_params=_cparams("parallel", "arbitrary"),
        name="attention",
    )(q, k, v, proj)


def _tile_masks():
    r = lax.broadcasted_iota(jnp.int32, (TILE, TILE), 0)
    c = lax.broadcasted_iota(jnp.int32, (TILE, TILE), 1)
    same = (r >> CHUNK_LOG2) == (c >> CHUNK_LOG2)
    return (same & (r >= c)).astype(F32), (same & (r <= c)).astype(F32)


def _col(x, lane):
    li = lax.broadcasted_iota(jnp.int32, x.shape, 1)
    col = jnp.sum(jnp.where(li == lane, x, 0.0), axis=1, keepdims=True)
    return jnp.broadcast_to(col, x.shape)


def _dup_chunks_t(cs):
    parts = []
    for c in range(TILE // CHUNK):
        blk = cs[c * CHUNK:(c + 1) * CHUNK, :]
        parts += [blk, blk]
    return jnp.concatenate(parts, axis=0).T


def _gdn_prep_kernel(q_ref, qp_ref, qn_ref, k_ref, kp_ref, kn_ref, v_ref, vp_ref, vn_ref,
                     cwq_ref, cwk_ref, cwv_ref, gate_ref, gpar_ref,
                     uf_ref, wf_ref, qdf_ref, kdf_ref, atf_ref,
                     ub_ref, wb_ref, qdb_ref, kdb_ref, atb_ref, egl_ref,
                     xs_ref, g_ref, csl_ref, csu_ref, cslt_ref, csut_ref):
    i = pl.program_id(1)
    nt = pl.num_programs(1)
    hg = pl.program_id(2)
    hps = q_ref.shape[1] // LANES

    @pl.when(hg == 0)
    def _():
        t = gate_ref[...]
        a_log = gpar_ref[0:1, :]
        dt_b = gpar_ref[1:2, :]
        li = lax.broadcasted_iota(jnp.int32, t.shape, 1)
        xg = t + dt_b
        sp = jnp.maximum(xg, 0.0) + jnp.log1p(jnp.exp(-jnp.abs(xg)))
        gdec = -jnp.exp(a_log) * sp
        beta = 1.0 / (1.0 + jnp.exp(-t))
        n_gate = 2 * GDN_HEADS
        g = jnp.where(li < n_gate, beta, jnp.where(li < 2 * n_gate, gdec, 0.0))
        g_ref[...] = g
        tri_l, tri_u = _tile_masks()
        hi = g.astype(BF16)
        r1 = g - hi.astype(F32)
        mid = r1.astype(BF16)
        lo = (r1 - mid.astype(F32)).astype(BF16)
        pieces = jnp.concatenate([hi, mid, lo], axis=1)

        def cumsum(tri):
            c3 = jnp.dot(tri.astype(BF16), pieces, preferred_element_type=F32)
            return c3[:, 0:LANES] + c3[:, LANES:2 * LANES] + c3[:, 2 * LANES:]

        csl = cumsum(tri_l)
        csu = cumsum(tri_u)
        csl_ref[...] = csl
        csu_ref[...] = csu
        cslt_ref[...] = _dup_chunks_t(csl)
        csut_ref[...] = _dup_chunks_t(csu)

    def conv_silu(main_ref, prev_ref, next_ref, w_ref):
        xs_ref[0:SUBLANES, :] = jnp.where(i > 0, prev_ref[...], 0.0)
        xs_ref[SUBLANES:SUBLANES + TILE, :] = main_ref[...]
        xs_ref[SUBLANES + TILE:, :] = jnp.where(i < nt - 1, next_ref[...], 0.0)
        pad = (CONV_K - 1) // 2
        xs = xs_ref[...]
        n = xs.shape[0]
        acc = None
        for tap in range(CONV_K):
            sh = xs if tap == pad else pltpu.roll(xs, (pad - tap) % n, axis=0)
            term = sh[SUBLANES:SUBLANES + TILE, :] * w_ref[tap:tap + 1, :]
            acc = term if acc is None else acc + term
        return _silu(acc)

    def l2n(x):
        return x * lax.rsqrt(jnp.sum(x * x, axis=-1, keepdims=True) + EPS)

    q_all = conv_silu(q_ref, qp_ref, qn_ref, cwq_ref)
    k_all = conv_silu(k_ref, kp_ref, kn_ref, cwk_ref)
    v_all = conv_silu(v_ref, vp_ref, vn_ref, cwv_ref)

    ri = lax.broadcasted_iota(jnp.int32, (CHUNK, LANES), 0)
    li = lax.broadcasted_iota(jnp.int32, (CHUNK, LANES), 1)
    lj = li & (CHUNK - 1)
    left = li < CHUNK
    left_f = left.astype(F32)
    eye_left = (ri == li).astype(F32)
    zeros_w = jnp.zeros((CHUNK, LANES), BF16)
    zeros_r = jnp.zeros((CHUNK, 2 * LANES), BF16)
    nt_dims = (((1,), (1,)), ((), ()))
    nchunk = TILE // CHUNK

    g = g_ref[...]
    csl = csl_ref[...]
    csu = csu_ref[...]
    nh = GDN_HEADS
    lower = (ri >= lj).astype(F32)
    slower = (ri > lj).astype(F32)
    upper = (ri <= lj).astype(F32)
    supper = (ri < lj).astype(F32)

    def setup_head(hh):
        h = hg * hps + hh
        cols = slice(hh * LANES, (hh + 1) * LANES)
        q = l2n(q_all[:, cols]) * GDN_SCALE
        k = l2n(k_all[:, cols])
        v = v_all[:, cols]
        qb = q.astype(BF16)
        kb16 = k.astype(BF16)
        grams = []
        for c in range(nchunk):
            rows = slice(c * CHUNK, (c + 1) * CHUNK)
            k2 = jnp.concatenate([kb16[rows], kb16[rows]], axis=0)
            grams.append((lax.dot_general(kb16[rows], k2, nt_dims, preferred_element_type=F32),
                          lax.dot_general(qb[rows], k2, nt_dims, preferred_element_type=F32)))

        def setup_direction(beta_lane, g_lane, cs, cs_other, cst_ref, tri, stri,
                            u_ref, w_ref, qd_ref, kd_ref, at_ref, egl_row):
            beta = _col(g, beta_lane)
            gc = _col(cs, g_lane)
            rest = _col(cs_other - g, g_lane)
            egc = jnp.exp(gc)
            qd_ref[0, :, cols] = (q * egc).astype(BF16)
            kd_ref[0, :, cols] = (k * jnp.exp(rest)).astype(BF16)
            vb = (v * beta).astype(BF16)
            kbg = (k * (beta * egc)).astype(BF16)
            tot = jnp.exp(gc + rest)
            gc_rows = cst_ref[pl.ds(g_lane, 1), :]
            chains = []
            for c in range(nchunk):
                rows = slice(c * CHUNK, (c + 1) * CHUNK)
                kk2, qk2 = grams[c]
                gc_row = gc_rows[:, c * LANES:(c + 1) * LANES]
                decay = jnp.exp(jnp.minimum(gc[rows] - gc_row, 0.0))
                x = jnp.where(left, eye_left, -(beta[rows] * kk2 * decay * stri))
                half = left_f if c % 2 == 0 else 1.0 - left_f
                at_ref[0, rows, cols] = (qk2 * decay * (tri * half)).astype(BF16)
                egl_ref[0, c, pl.ds(egl_row, 1), :] = tot[c * CHUNK:c * CHUNK + 1, :]
                rhs = jnp.concatenate(
                    [jnp.concatenate([vb[rows], kbg[rows]], axis=1), zeros_r], axis=0)
                chains.append((x, rhs, rows, cols, u_ref, w_ref))
            return chains

        return (setup_direction(h, 2 * nh + h, csl, csu, cslt_ref, lower, slower,
                                uf_ref, wf_ref, qdf_ref, kdf_ref, atf_ref, h)
                + setup_direction(nh + h, 3 * nh + h, csu, csl, csut_ref, upper, supper,
                                  ub_ref, wb_ref, qdb_ref, kdb_ref, atb_ref, nh + h))

    chains = []
    for hh in range(hps):
        chains += setup_head(hh)
    xs = [ch[0] for ch in chains]
    for _ in range(CHUNK_LOG2):
        nxt = []
        for x in xs:
            xb = x.astype(BF16)
            nxt.append(x * left_f + jnp.dot(xb, jnp.concatenate([zeros_w, xb], axis=0),
                                            preferred_element_type=F32))
        xs = nxt
    for x, (_, rhs, rows, cols, u_ref, w_ref) in zip(xs, chains):
        uw = jnp.dot(x.astype(BF16), rhs, preferred_element_type=F32)
        u_ref[0, rows, cols] = uw[:, 0:GDN_DV].astype(BF16)
        w_ref[0, rows, cols] = uw[:, GDN_DV:].astype(BF16)


def _gdn_prep(proj, conv_w, gpar, b, s, hps):
    nt = s // TILE
    hs = GDN_HEADS
    rows8 = TILE // SUBLANES
    last8 = b * s // SUBLANES - 1
    wid = hps * LANES
    ng = hs // hps

    def main(col0):
        return pl.BlockSpec((TILE, wid), lambda bi, i, h: (bi * nt + i, col0 + h))

    def prev(col0):
        return pl.BlockSpec((SUBLANES, wid),
                            lambda bi, i, h: (jnp.maximum((bi * nt + i) * rows8 - 1, 0), col0 + h))

    def nxt(col0):
        return pl.BlockSpec((SUBLANES, wid),
                            lambda bi, i, h: (jnp.minimum((bi * nt + i + 1) * rows8, last8), col0 + h))

    def cw(col0):
        return pl.BlockSpec((CONV_K, wid), lambda bi, i, h: (0, col0 + h))

    qc, kc, vc = 0, ng, 2 * ng
    seq = jax.ShapeDtypeStruct((b, s, GDN_WIDTH), BF16)
    seq_spec = pl.BlockSpec((1, TILE, wid), lambda bi, i, h: (bi, i, h))
    nchunk = TILE // CHUNK
    egl = jax.ShapeDtypeStruct((b * nt, nchunk, 2 * hs, LANES), F32)
    egl_spec = pl.BlockSpec((1, nchunk, 2 * hs, LANES), lambda bi, i, h: (bi * nt + i, 0, 0, 0))
    return pl.pallas_call(
        _gdn_prep_kernel,
        out_shape=(seq,) * 10 + (egl,),
        grid=(b, nt, ng),
        in_specs=[main(qc), prev(qc), nxt(qc), main(kc), prev(kc), nxt(kc),
                  main(vc), prev(vc), nxt(vc), cw(qc), cw(kc), cw(vc),
                  pl.BlockSpec((TILE, LANES), lambda bi, i, h: (bi * nt + i, COL_GATE // LANES)),
                  pl.BlockSpec((SUBLANES, LANES), lambda bi, i, h: (0, 0))],
        out_specs=(seq_spec,) * 10 + (egl_spec,),
        scratch_shapes=[pltpu.VMEM((TILE + 2 * SUBLANES, wid), F32),
                        pltpu.VMEM((TILE, LANES), F32), pltpu.VMEM((TILE, LANES), F32),
                        pltpu.VMEM((TILE, LANES), F32), pltpu.VMEM((LANES, 2 * TILE), F32),
                        pltpu.VMEM((LANES, 2 * TILE), F32)],
        compiler_params=_cparams("parallel", "parallel", "arbitrary"),
        name="gdn_prep",
    )(*([proj] * 9), conv_w, conv_w, conv_w, proj, gpar)


def _gdn_scan_kernel(uf_ref, wf_ref, qdf_ref, kdf_ref, atf_ref, eglf_ref,
                     ub_ref, wb_ref, qdb_ref, kdb_ref, atb_ref, eglb_ref,
                     of_ref, ob_ref, st_ref):
    @pl.when(pl.program_id(1) == 0)
    def _():
        st_ref[...] = jnp.zeros_like(st_ref)

    nchunk = TILE // CHUNK
    tn_dims = (((0,), (0,)), ((), ()))

    fwd_refs = (uf_ref, wf_ref, qdf_ref, kdf_ref, atf_ref, eglf_ref, of_ref)
    bwd_refs = (ub_ref, wb_ref, qdb_ref, kdb_ref, atb_ref, eglb_ref, ob_ref)

    def step(c, carry):
        chains = []
        for hd in range(GDN_HEADS):
            chains.append((hd, c, hd, fwd_refs))
            chains.append((GDN_HEADS + hd, nchunk - 1 - c, hd, bwd_refs))
        stage1 = []
        for idx, cc, hd, (u_ref, w_ref, qd_ref, kd_ref, at_ref, egl_ref, o_ref) in chains:
            rows = pl.ds(pl.multiple_of(cc * CHUNK, CHUNK), CHUNK)
            cols = slice(hd * LANES, (hd + 1) * LANES)
            sb = st_ref[idx].astype(BF16)
            ws = jnp.dot(w_ref[0, rows, cols], sb, preferred_element_type=F32)
            qs = jnp.dot(qd_ref[0, rows, cols], sb, preferred_element_type=F32)
            stage1.append((rows, cols, ws, qs))
        for (idx, cc, hd, refs), (rows, cols, ws, qs) in zip(chains, stage1):
            u_ref, w_ref, qd_ref, kd_ref, at_ref, egl_ref, o_ref = refs
            v_new = (u_ref[0, rows, cols].astype(F32) - ws).astype(BF16)
            v_pair = jnp.concatenate([v_new, v_new], axis=0)
            o_ref[0, rows, cols] = (qs + jnp.dot(at_ref[0, rows, cols], v_pair,
                                                 preferred_element_type=F32)).astype(o_ref.dtype)
            decay = egl_ref[0, pl.ds(cc, 1), idx, :]
            upd = lax.dot_general(kd_ref[0, rows, cols], v_new, tn_dims,
                                  preferred_element_type=F32)
            st_ref[idx] = st_ref[idx] * decay + upd
        return carry

    lax.fori_loop(0, nchunk, step, 0)


def _gdn_scan(prep, b, s):
    uf, wf, qdf, kdf, atf, ub, wb, qdb, kdb, atb, egl = prep
    nt = s // TILE
    hs = GDN_HEADS
    nchunk = TILE // CHUNK
    fwd = pl.BlockSpec((1, TILE, GDN_WIDTH), lambda bi, i: (bi, i, 0))
    bwd = pl.BlockSpec((1, TILE, GDN_WIDTH), lambda bi, i: (bi, nt - 1 - i, 0))
    egl_f = pl.BlockSpec((1, nchunk, 2 * hs, LANES), lambda bi, i: (bi * nt + i, 0, 0, 0))
    egl_b = pl.BlockSpec((1, nchunk, 2 * hs, LANES), lambda bi, i: (bi * nt + nt - 1 - i, 0, 0, 0))
    out = jax.ShapeDtypeStruct((b, s, GDN_WIDTH), BF16)
    return pl.pallas_call(
        _gdn_scan_kernel,
        out_shape=(out, out),
        grid=(b, nt),
        in_specs=[fwd] * 5 + [egl_f] + [bwd] * 5 + [egl_b],
        out_specs=(fwd, bwd),
        scratch_shapes=[pltpu.VMEM((2 * hs, GDN_DK, GDN_DV), F32)],
        compiler_params=_cparams("parallel", "arbitrary"),
        name="gdn_scan",
    )(uf, wf, qdf, kdf, atf, egl, ub, wb, qdb, kdb, atb, egl)


def _outproj_kernel(of_ref, ob_ref, za_ref, mixb_ref, x_ref, gn_ref, gp_ref, w_ref, y_ref):
    o = of_ref[...].astype(F32) + ob_ref[...].astype(F32)
    za = za_ref[...]
    parts = []
    for h in range(GDN_HEADS):
        sl = slice(h * GDN_DV, (h + 1) * GDN_DV)
        parts.append((_rms(o[:, sl], gn_ref[...]) * _silu(za[:, sl])).astype(BF16))
    mix_a = jnp.concatenate(parts, axis=1)
    y = jnp.dot(mix_a, w_ref[0:GDN_WIDTH, :], preferred_element_type=F32)
    y = y + jnp.dot(mixb_ref[...], w_ref[GDN_WIDTH:, :], preferred_element_type=F32)
    y_ref[...] = x_ref[...] + _rms(y, gp_ref[...])


def _outproj(o_f, o_b, proj, mix_b, x2d, gn, gp, w_out, tm):
    t = x2d.shape[0]
    row = lambda i: (i, 0)
    const = lambda i: (0, 0)
    return pl.pallas_call(
        _outproj_kernel,
        out_shape=jax.ShapeDtypeStruct((t, D_MODEL), F32),
        grid=(t // tm,),
        in_specs=[
            pl.BlockSpec((tm, GDN_WIDTH), row),
            pl.BlockSpec((tm, GDN_WIDTH), row),
            pl.BlockSpec((tm, GDN_WIDTH), lambda i: (i, COL_ZA // GDN_WIDTH)),
            pl.BlockSpec((tm, MLA_WIDTH), row),
            pl.BlockSpec((tm, D_MODEL), row),
            pl.BlockSpec((1, GDN_DV), const),
            pl.BlockSpec((1, D_MODEL), const),
            pl.BlockSpec((D_MIX, D_MODEL), const),
        ],
        out_specs=pl.BlockSpec((tm, D_MODEL), row),
        compiler_params=_cparams("parallel"),
        name="outproj",
    )(o_f, o_b, proj, mix_b, x2d, gn, gp, w_out)


def _swap_halves(w):
    half = MLA_ROPE // 2
    return jnp.concatenate([w[..., half:], w[..., :half]], axis=-1)


def _prep_layer(w_in, mla_w_uq, mla_w_ukv, w_out, a_log, dt_bias):
    o = np.cumsum((0, GDN_QKV, GDN_WIDTH, 2 * GDN_HEADS, 2 * GDN_HEADS, Q_LORA, KV_LORA,
                   MLA_ROPE, MLA_WIDTH))
    qkv, za, bl, al, cq, ckv, kpe, zb = (w_in[:, o[n]:o[n + 1]] for n in range(8))
    pad = jnp.zeros((D_MODEL, LANES - 4 * GDN_HEADS), w_in.dtype)
    w_pad = jnp.concatenate([qkv, za, zb, cq, ckv, kpe, _swap_halves(kpe), bl, al, pad],
                            axis=1).astype(BF16)
    wq = mla_w_uq.reshape(Q_LORA, MLA_HEADS, MLA_NOPE + MLA_ROPE)
    wq_ext = jnp.concatenate([wq, _swap_halves(wq[..., MLA_NOPE:])], axis=-1)
    wq_ext = wq_ext.reshape(Q_LORA, MLA_HEADS * QK_PAD).astype(BF16)
    wkv = mla_w_ukv.astype(BF16)
    gpar = jnp.zeros((SUBLANES, LANES), F32)
    n_gate = 2 * GDN_HEADS
    gpar = gpar.at[0, n_gate:2 * n_gate].set(a_log.reshape(-1))
    gpar = gpar.at[1, n_gate:2 * n_gate].set(dt_bias.reshape(-1))
    return w_pad, wq_ext, wkv, w_out.astype(BF16), gpar


def _rope_tables(s):
    pos = jnp.arange(s, dtype=F32)
    inv = ROPE_BASE ** (-jnp.arange(0, MLA_ROPE, 2, dtype=F32) / MLA_ROPE)
    ang = pos[:, None] * inv[None, :]
    cos, sin = jnp.cos(ang), jnp.sin(ang)
    zero = jnp.zeros((s, LANES - MLA_ROPE), F32)
    return (jnp.concatenate([cos, cos, zero], axis=1),
            jnp.concatenate([-sin, sin, zero], axis=1))


def _pick(n, prefs):
    for p in prefs:
        if n % p == 0:
            return p
    return n


def _tiles(b, s):
    t = b * s
    return dict(tm_in=_pick(t, (1024, 512, 256)), tn_in=1024,
                ts_mla=_pick(s, (512, 256)),
                tq=_pick(s, (512, 256, 128)), tk=_pick(s // 2, (1024, 512, 256, 128)),
                tm_out=_pick(t, (512, 256)), gdn_hps=4)


def _layer(x2d, b, s, tl, pre_g, post_g, conv_w, gdn_norm_g, q_norm_g, kv_norm_g,
           w_pad, wq_ext, wkv, w_out, gpar, cos2, sin2):
    proj = _inproj(x2d, pre_g.reshape(1, -1), w_pad, tl["tm_in"], tl["tn_in"])
    prep = _gdn_prep(proj, conv_w, gpar, b, s, tl["gdn_hps"])
    o_f, o_b = _gdn_scan(prep, b, s)
    q, k, v = _mla_proj(proj, q_norm_g.reshape(1, -1), kv_norm_g.reshape(1, -1), wq_ext, wkv,
                        cos2, sin2, b, s, tl["ts_mla"])
    mix_b = _attention(q, k, v, proj, b, s, tl["tq"], tl["tk"])
    return _outproj(o_f.reshape(b * s, -1), o_b.reshape(b * s, -1), proj,
                    mix_b.reshape(b * s, -1), x2d, gdn_norm_g.reshape(1, -1),
                    post_g.reshape(1, -1), w_out, tl["tm_out"])


def _trunk(x, layers, pre_norm_g, post_norm_g, conv_w, gdn_norm_g, mla_q_norm_g, mla_kv_norm_g):
    b, s, d = x.shape
    assert d == D_MODEL and s % TILE == 0
    tl = _tiles(b, s)
    cos2, sin2 = _rope_tables(s)
    x2d = x.reshape(b * s, d)
    for l in range(DEPTH):
        x2d = _layer(x2d, b, s, tl, pre_norm_g[l], post_norm_g[l], conv_w[l], gdn_norm_g[l],
                     mla_q_norm_g[l], mla_kv_norm_g[l], *layers[l], cos2, sin2)
    return x2d.reshape(b, s, d)


def kernel(x_prompt, x_sample, pre_norm_g, post_norm_g, w_in, conv_w, gdn_a_log, gdn_dt_bias,
           gdn_norm_g, mla_q_norm_g, mla_kv_norm_g, mla_w_uq, mla_w_ukv, w_out):
    layers = [_prep_layer(w_in[l], mla_w_uq[l], mla_w_ukv[l], w_out[l], gdn_a_log[l],
                          gdn_dt_bias[l]) for l in range(DEPTH)]
    args = (layers, pre_norm_g, post_norm_g, conv_w, gdn_norm_g, mla_q_norm_g, mla_kv_norm_g)
    return (_trunk(x_prompt, *args), _trunk(x_sample, *args))
```

```python
import functools

import numpy as np
import jax
import jax.numpy as jnp
from jax import lax
from jax.experimental import pallas as pl
from jax.experimental.pallas import tpu as pltpu

F32 = jnp.float32
BF16 = jnp.bfloat16

D_MODEL = 2048
DEPTH = 2
GDN_HEADS = 8
GDN_DK = 128
GDN_DV = 128
GDN_WIDTH = GDN_HEADS * GDN_DV
GDN_QKV = 2 * GDN_HEADS * GDN_DK + GDN_WIDTH
CONV_K = 5
CHUNK = 64
MLA_HEADS = 8
MLA_NOPE = 128
MLA_ROPE = 64
MLA_DV = 128
MLA_WIDTH = MLA_HEADS * MLA_DV
Q_LORA = 512
KV_LORA = 256
ROPE_BASE = 10000.0
D_MIX = GDN_WIDTH + MLA_WIDTH
EPS = 1e-6
MLA_SCALE = (MLA_NOPE + MLA_ROPE) ** -0.5
GDN_SCALE = GDN_DK ** -0.5
LOG2_E = float(np.log2(np.e))

LANES = 128
SUBLANES = 8
VMEM_LIMIT = 56 * 1024 * 1024

COL_ZA = 0
COL_ZB = COL_ZA + GDN_WIDTH
COL_CQ = COL_ZB + MLA_WIDTH
COL_CKV = COL_CQ + Q_LORA
COL_KPE = COL_CKV + KV_LORA
COL_GATE = COL_KPE + 2 * MLA_ROPE
D_REST = COL_GATE + LANES
D_PROJ = GDN_QKV + D_REST
MISC_W = D_REST - COL_CQ
QK_PAD = 2 * LANES
V_PAD = 2 * LANES
QK_SCALE_LOG2 = MLA_SCALE * LOG2_E
TILE = 4 * CHUNK
HALO = 2 * SUBLANES
CHUNK_LOG2 = CHUNK.bit_length() - 1
assert 1 << CHUNK_LOG2 == CHUNK and 2 * CHUNK == LANES

assert D_PROJ == 6144 and D_REST == 3072 and MISC_W == 1024 and COL_GATE - COL_CQ == 896


def _cparams(*sem):
    return pltpu.CompilerParams(dimension_semantics=sem, vmem_limit_bytes=VMEM_LIMIT)


def _silu(z):
    return z / (1.0 + jnp.exp(-z))


def _rms(x, g):
    return x * lax.rsqrt(jnp.mean(x * x, axis=-1, keepdims=True) + EPS) * g


def _inproj_kernel(x_ref, xp_ref, xn_ref, g_ref, w_ref, cw_ref, qkv_ref, rest_ref,
                   h_ref, *y_refs, tiles_per_seq, chunk, n_conv):
    i = pl.program_id(0)
    j = pl.program_id(1)
    tm, tn = rest_ref.shape
    pad = (CONV_K - 1) // 2
    win = chunk + 2 * SUBLANES

    @pl.when(j == 0)
    def _():
        g = g_ref[...]
        first = (i % tiles_per_seq) == 0
        last = (i % tiles_per_seq) == tiles_per_seq - 1
        h_ref[0:HALO, :] = jnp.where(first, 0.0, _rms(xp_ref[...], g)).astype(BF16)
        h_ref[HALO:HALO + tm, :] = _rms(x_ref[...], g).astype(BF16)
        h_ref[HALO + tm:, :] = jnp.where(last, 0.0, _rms(xn_ref[...], g)).astype(BF16)

    def conv_silu(y_ref, c, cols):
        ys = y_ref[HALO + c * chunk - SUBLANES:HALO + (c + 1) * chunk + SUBLANES, cols]
        acc = None
        for tap in range(CONV_K):
            sh = ys if tap == pad else pltpu.roll(ys, (pad - tap) % win, axis=0)
            term = sh[SUBLANES:SUBLANES + chunk, :] * cw_ref[tap:tap + 1, cols]
            acc = term if acc is None else acc + term
        return _silu(acc)

    def finish_units(y_ref, tile, half):
        section = tile // (GDN_WIDTH // tn)
        n_c = tm // chunk

        def unit(c, hd):
            rows = slice(c * chunk, (c + 1) * chunk)
            cols = slice(hd * LANES, (hd + 1) * LANES)
            x = conv_silu(y_ref, c, cols)
            if section < 2:
                scale = GDN_SCALE if section == 0 else 1.0
                x = x * (lax.rsqrt(jnp.sum(x * x, axis=-1, keepdims=True) + EPS) * scale)
            qkv_ref[rows, cols] = x

        return [functools.partial(unit, c, hd)
                for c in range(half * n_c // 2, (half + 1) * n_c // 2)
                for hd in range(tn // LANES)]

    for step in range(2 * n_conv + 1):
        @pl.when(j == step)
        def _(step=step):
            units = []
            if step > 0:
                units = finish_units(y_refs[(step - 1) // 2], (step - 1) // 2, (step - 1) % 2)
            if step < n_conv:
                dst, base = y_refs[step], 0
                cuts = [0] + [HALO + (r + 1) * (tm // 4) for r in range(3)] + [tm + 2 * HALO]
            elif step < 2 * n_conv:
                dst, base = rest_ref, HALO
                cuts = [r * (tm // 4) for r in range(5)]
            else:
                dst, cuts = None, [0]
            per = -(-len(units) // max(len(cuts) - 1, 1))
            for r in range(len(cuts) - 1):
                dst[cuts[r]:cuts[r + 1], :] = jnp.dot(
                    h_ref[base + cuts[r]:base + cuts[r + 1], :], w_ref[...],
                    preferred_element_type=F32)
                for unit in units[r * per:(r + 1) * per]:
                    unit()
            for unit in units[(len(cuts) - 1) * per:]:
                unit()


def _inproj(x2d, g, w_pad, conv_w, s, tm, tn):
    t = x2d.shape[0]
    n_conv = GDN_QKV // tn
    assert s % tm == 0 and GDN_WIDTH % tn == 0 and D_REST == GDN_QKV and tm % (2 * TILE) == 0
    nh = tm // HALO
    last = t // HALO - 1
    n_col = D_PROJ // tn

    def conv_tile(j):
        return jnp.clip((j - 1) // 2, 0, n_conv - 1)

    return pl.pallas_call(
        functools.partial(_inproj_kernel, tiles_per_seq=s // tm, chunk=TILE, n_conv=n_conv),
        out_shape=(jax.ShapeDtypeStruct((t, GDN_QKV), F32), jax.ShapeDtypeStruct((t, D_REST), F32)),
        grid=(t // tm, n_col + 1),
        in_specs=[
            pl.BlockSpec((tm, D_MODEL), lambda i, j: (i, 0)),
            pl.BlockSpec((HALO, D_MODEL), lambda i, j: (jnp.maximum(i * nh - 1, 0), 0)),
            pl.BlockSpec((HALO, D_MODEL), lambda i, j: (jnp.minimum((i + 1) * nh, last), 0)),
            pl.BlockSpec((1, D_MODEL), lambda i, j: (0, 0)),
            pl.BlockSpec((D_MODEL, tn), lambda i, j: (0, jnp.minimum(j, n_col - 1))),
            pl.BlockSpec((CONV_K, tn), lambda i, j: (0, conv_tile(j))),
        ],
        out_specs=(pl.BlockSpec((tm, tn), lambda i, j: (i, conv_tile(j))),
                   pl.BlockSpec((tm, tn), lambda i, j: (i, jnp.clip(j - n_conv, 0, n_conv - 1)))),
        scratch_shapes=[pltpu.VMEM((tm + 2 * HALO, D_MODEL), BF16)]
        + [pltpu.VMEM((tm + 2 * HALO, tn), F32)] * n_conv,
        compiler_params=_cparams("parallel", "arbitrary"),
        name="inproj",
    )(x2d, x2d, x2d, g, w_pad, conv_w)


def _mla_proj_kernel(p_ref, gq_ref, gkv_ref, wq_ref, wkv_ref, cos_ref, sin_ref,
                     q_ref, k_ref, v_ref):
    t = p_ref[...]
    cqn = _rms(t[:, 0:Q_LORA], gq_ref[...]).astype(BF16)
    ckn = _rms(t[:, Q_LORA:Q_LORA + KV_LORA], gkv_ref[...]).astype(BF16)
    kpe2 = t[:, COL_KPE - COL_CQ:COL_GATE - COL_CQ]
    qe = jnp.dot(cqn, wq_ref[...], preferred_element_type=F32)
    kve = jnp.dot(ckn, wkv_ref[...], preferred_element_type=F32)
    c2 = cos_ref[...]
    s2 = sin_ref[...]
    krope = (kpe2 * c2 + pltpu.roll(kpe2, MLA_ROPE, axis=1) * s2).astype(BF16)
    ones = jnp.ones((t.shape[0], V_PAD - MLA_DV), BF16)
    for h in range(MLA_HEADS):
        qn = qe[:, h * QK_PAD:h * QK_PAD + LANES]
        qp = qe[:, h * QK_PAD + LANES:(h + 1) * QK_PAD]
        qr = qp * c2 + pltpu.roll(qp, MLA_ROPE, axis=1) * s2
        q_ref[0, h, :, 0:LANES] = (qn * QK_SCALE_LOG2).astype(BF16)
        q_ref[0, h, :, LANES:QK_PAD] = (qr * QK_SCALE_LOG2).astype(BF16)
        k_ref[0, h, :, 0:LANES] = kve[:, h * QK_PAD:h * QK_PAD + LANES].astype(BF16)
        k_ref[0, h, :, LANES:QK_PAD] = krope
        v_ref[0, h, :, 0:MLA_DV] = kve[:, h * QK_PAD + LANES:(h + 1) * QK_PAD].astype(BF16)
        v_ref[0, h, :, MLA_DV:] = ones


def _mla_proj(proj, gq, gkv, wq_ext, wkv, cos2, sin2, b, s, ts):
    nt = s // ts
    hs = MLA_HEADS
    return pl.pallas_call(
        _mla_proj_kernel,
        out_shape=(jax.ShapeDtypeStruct((b, hs, s, QK_PAD), BF16),
                   jax.ShapeDtypeStruct((b, hs, s, QK_PAD), BF16),
                   jax.ShapeDtypeStruct((b, hs, s, V_PAD), BF16)),
        grid=(b, nt),
        in_specs=[
            pl.BlockSpec((ts, MISC_W), lambda bi, i: (bi * nt + i, COL_CQ // MISC_W)),
            pl.BlockSpec((1, Q_LORA), lambda bi, i: (0, 0)),
            pl.BlockSpec((1, KV_LORA), lambda bi, i: (0, 0)),
            pl.BlockSpec((Q_LORA, hs * QK_PAD), lambda bi, i: (0, 0)),
            pl.BlockSpec((KV_LORA, hs * QK_PAD), lambda bi, i: (0, 0)),
            pl.BlockSpec((ts, LANES), lambda bi, i: (i, 0)),
            pl.BlockSpec((ts, LANES), lambda bi, i: (i, 0)),
        ],
        out_specs=(pl.BlockSpec((1, hs, ts, QK_PAD), lambda bi, i: (bi, 0, i, 0)),
                   pl.BlockSpec((1, hs, ts, QK_PAD), lambda bi, i: (bi, 0, i, 0)),
                   pl.BlockSpec((1, hs, ts, V_PAD), lambda bi, i: (bi, 0, i, 0))),
        compiler_params=_cparams("parallel", "parallel"),
        name="mla_proj",
    )(proj, gq, gkv, wq_ext, wkv, cos2, sin2)


def _attn_kernel(q_ref, k_ref, v_ref, z_ref, o_ref, sa_ref, sb_ref, p_ref, m_ref, acc_ref,
                 *, tq, tk, rb):
    s_len = k_ref.shape[2]
    nq, nk = s_len // tq, s_len // tk
    nt_dims = (((1,), (1,)), ((), ()))
    blocks = [slice(r * rb, (r + 1) * rb) for r in range(tq // rb)]
    lane_tiles = [slice(t * LANES, (t + 1) * LANES) for t in range(tk // LANES)]

    def scores(i, j, s_ref):
        q = q_ref[0, 0, pl.ds(pl.multiple_of(i * tq, tq), tq), :]
        s_ref[...] = lax.dot_general(q, k_ref[0, 0, j * tk:(j + 1) * tk, :], nt_dims,
                                     preferred_element_type=F32)

    def softmax_and_pv(j, s_ref):
        mx_parts = []
        for rows in blocks:
            mx = s_ref[rows, lane_tiles[0]]
            for lt in lane_tiles[1:]:
                mx = jnp.maximum(mx, s_ref[rows, lt])
            mx_parts.append(mx)
        mx_all = jnp.concatenate(mx_parts, axis=0)
        row_max = jnp.broadcast_to(jnp.max(mx_all, axis=-1, keepdims=True), mx_all.shape)
        if j == 0:
            m_new = row_max
        else:
            m_old = m_ref[...]
            m_new = jnp.maximum(m_old, row_max)
            alpha = jnp.exp2(m_old - m_new)
        m_ref[...] = m_new
        for rows in blocks:
            m_b = m_new[rows]
            for lt in lane_tiles:
                p_ref[rows, lt] = jnp.exp2(s_ref[rows, lt] - m_b).astype(BF16)
        pv = jnp.dot(p_ref[...], v_ref[0, 0, j * tk:(j + 1) * tk, :],
                     preferred_element_type=F32)
        if j == 0:
            acc_ref[...] = pv
        else:
            acc_ref[...] = acc_ref[...] * jnp.concatenate([alpha, alpha], axis=1) + pv

    bufs = (sa_ref, sb_ref)
    scores(0, 0, sa_ref)

    def body(i, carry):
        for j in range(nk):
            if j + 1 < nk:
                scores(i, j + 1, bufs[(j + 1) % 2])
            else:
                scores(jnp.minimum(i + 1, nq - 1), 0, bufs[0])
            softmax_and_pv(j, bufs[j % 2])
        rows = pl.ds(pl.multiple_of(i * tq, tq), tq)
        acc = acc_ref[...]
        o_ref[0, rows, :] = ((acc[:, 0:MLA_DV] / acc[:, MLA_DV:])
                             * _silu(z_ref[rows, :])).astype(BF16)
        return carry

    lax.fori_loop(0, nq, body, 0)


def _attention(q, k, v, proj, b, s, tq, tk):
    hs = MLA_HEADS
    zb0 = COL_ZB // LANES
    assert (s // tk) % 2 == 0
    return pl.pallas_call(
        functools.partial(_attn_kernel, tq=tq, tk=tk, rb=2 * SUBLANES),
        out_shape=jax.ShapeDtypeStruct((b, s, MLA_WIDTH), BF16),
        grid=(b, hs),
        in_specs=[
            pl.BlockSpec((1, 1, s, QK_PAD), lambda bi, h: (bi, h, 0, 0)),
            pl.BlockSpec((1, 1, s, QK_PAD), lambda bi, h: (bi, h, 0, 0)),
            pl.BlockSpec((1, 1, s, V_PAD), lambda bi, h: (bi, h, 0, 0)),
            pl.BlockSpec((s, LANES), lambda bi, h: (bi, zb0 + h)),
        ],
        out_specs=pl.BlockSpec((1, s, MLA_DV), lambda bi, h: (bi, 0, h)),
        scratch_shapes=[pltpu.VMEM((tq, tk), F32), pltpu.VMEM((tq, tk), F32),
                        pltpu.VMEM((tq, tk), BF16),
                        pltpu.VMEM((tq, LANES), F32), pltpu.VMEM((tq, V_PAD), F32)],
        compiler_params=_cparams("parallel", "arbitrary"),
        name="attention",
    )(q, k, v, proj)


def _tile_masks():
    r = lax.broadcasted_iota(jnp.int32, (TILE, TILE), 0)
    c = lax.broadcasted_iota(jnp.int32, (TILE, TILE), 1)
    same = (r >> CHUNK_LOG2) == (c >> CHUNK_LOG2)
    return (same & (r >= c)).astype(F32), (same & (r <= c)).astype(F32)


def _col(x, lane):
    li = lax.broadcasted_iota(jnp.int32, x.shape, 1)
    col = jnp.sum(jnp.where(li == lane, x, 0.0), axis=1, keepdims=True)
    return jnp.broadcast_to(col, x.shape)


def _dup_chunks_t(cs):
    parts = []
    for c in range(TILE // CHUNK):
        blk = cs[c * CHUNK:(c + 1) * CHUNK, :]
        parts += [blk, blk]
    return jnp.concatenate(parts, axis=0).T


def _gdn_prep_kernel(q_ref, k_ref, v_ref, gate_ref, gpar_ref,
                     uf_ref, wf_ref, qdf_ref, kdf_ref, atf_ref,
                     ub_ref, wb_ref, qdb_ref, kdb_ref, atb_ref, egl_ref,
                     g_ref, csl_ref, csu_ref, cslt_ref, csut_ref):
    hg = pl.program_id(2)
    hps = q_ref.shape[1] // LANES

    @pl.when(hg == 0)
    def _():
        t = gate_ref[...]
        a_log = gpar_ref[0:1, :]
        dt_b = gpar_ref[1:2, :]
        li = lax.broadcasted_iota(jnp.int32, t.shape, 1)
        xg = t + dt_b
        sp = jnp.maximum(xg, 0.0) + jnp.log1p(jnp.exp(-jnp.abs(xg)))
        gdec = -jnp.exp(a_log) * sp
        beta = 1.0 / (1.0 + jnp.exp(-t))
        n_gate = 2 * GDN_HEADS
        g = jnp.where(li < n_gate, beta, jnp.where(li < 2 * n_gate, gdec, 0.0))
        g_ref[...] = g
        tri_l, tri_u = _tile_masks()
        hi = g.astype(BF16)
        r1 = g - hi.astype(F32)
        mid = r1.astype(BF16)
        lo = (r1 - mid.astype(F32)).astype(BF16)
        pieces = jnp.concatenate([hi, mid, lo], axis=1)

        def cumsum(tri):
            c3 = jnp.dot(tri.astype(BF16), pieces, preferred_element_type=F32)
            return c3[:, 0:LANES] + c3[:, LANES:2 * LANES] + c3[:, 2 * LANES:]

        csl = cumsum(tri_l)
        csu = cumsum(tri_u)
        csl_ref[...] = csl
        csu_ref[...] = csu
        cslt_ref[...] = _dup_chunks_t(csl)
        csut_ref[...] = _dup_chunks_t(csu)

    ri = lax.broadcasted_iota(jnp.int32, (CHUNK, LANES), 0)
    li = lax.broadcasted_iota(jnp.int32, (CHUNK, LANES), 1)
    lj = li & (CHUNK - 1)
    left = li < CHUNK
    left_f = left.astype(F32)
    eye_left = (ri == li).astype(F32)
    zeros_w = jnp.zeros((CHUNK, LANES), BF16)
    zeros_r = jnp.zeros((CHUNK, 2 * LANES), BF16)
    nt_dims = (((1,), (1,)), ((), ()))
    nchunk = TILE // CHUNK

    g = g_ref[...]
    csl = csl_ref[...]
    csu = csu_ref[...]
    nh = GDN_HEADS
    lower = (ri >= lj).astype(F32)
    slower = (ri > lj).astype(F32)
    upper = (ri <= lj).astype(F32)
    supper = (ri < lj).astype(F32)

    def setup_head(hh):
        h = hg * hps + hh
        cols = slice(hh * LANES, (hh + 1) * LANES)
        q = q_ref[:, cols]
        k = k_ref[:, cols]
        v = v_ref[:, cols]
        qb = q.astype(BF16)
        kb16 = k.astype(BF16)
        grams = []
        for c in range(nchunk):
            rows = slice(c * CHUNK, (c + 1) * CHUNK)
            k2 = jnp.concatenate([kb16[rows], kb16[rows]], axis=0)
            grams.append((lax.dot_general(kb16[rows], k2, nt_dims, preferred_element_type=F32),
                          lax.dot_general(qb[rows], k2, nt_dims, preferred_element_type=F32)))

        def setup_direction(beta_lane, g_lane, cs, cs_other, cst_ref, tri, stri,
                            u_ref, w_ref, qd_ref, kd_ref, at_ref, egl_row):
            beta = _col(g, beta_lane)
            gc = _col(cs, g_lane)
            rest = _col(cs_other - g, g_lane)
            egc = jnp.exp(gc)
            qd_ref[0, :, cols] = (q * egc).astype(BF16)
            kd_ref[0, :, cols] = (k * jnp.exp(rest)).astype(BF16)
            vb = (v * beta).astype(BF16)
            kbg = (k * (beta * egc)).astype(BF16)
            tot = jnp.exp(gc + rest)
            gc_rows = cst_ref[pl.ds(g_lane, 1), :]
            chains = []
            for c in range(nchunk):
                rows = slice(c * CHUNK, (c + 1) * CHUNK)
                kk2, qk2 = grams[c]
                gc_row = gc_rows[:, c * LANES:(c + 1) * LANES]
                decay = jnp.exp(jnp.minimum(gc[rows] - gc_row, 0.0))
                x = jnp.where(left, eye_left, -(beta[rows] * kk2 * decay * stri))
                half = left_f if c % 2 == 0 else 1.0 - left_f
                at_ref[0, rows, cols] = (qk2 * decay * (tri * half)).astype(BF16)
                egl_ref[0, c, pl.ds(egl_row, 1), :] = tot[c * CHUNK:c * CHUNK + 1, :]
                rhs = jnp.concatenate(
                    [jnp.concatenate([vb[rows], kbg[rows]], axis=1), zeros_r], axis=0)
                chains.append((x, rhs, rows, cols, u_ref, w_ref))
            return chains

        return (setup_direction(h, 2 * nh + h, csl, csu, cslt_ref, lower, slower,
                                uf_ref, wf_ref, qdf_ref, kdf_ref, atf_ref, h)
                + setup_direction(nh + h, 3 * nh + h, csu, csl, csut_ref, upper, supper,
                                  ub_ref, wb_ref, qdb_ref, kdb_ref, atb_ref, nh + h))

    chains = []
    for hh in range(hps):
        chains += setup_head(hh)
    xs = [ch[0] for ch in chains]
    for _ in range(CHUNK_LOG2):
        nxt = []
        for x in xs:
            xb = x.astype(BF16)
            nxt.append(x * left_f + jnp.dot(xb, jnp.concatenate([zeros_w, xb], axis=0),
                                            preferred_element_type=F32))
        xs = nxt
    for x, (_, rhs, rows, cols, u_ref, w_ref) in zip(xs, chains):
        uw = jnp.dot(x.astype(BF16), rhs, preferred_element_type=F32)
        u_ref[0, rows, cols] = uw[:, 0:GDN_DV].astype(BF16)
        w_ref[0, rows, cols] = uw[:, GDN_DV:].astype(BF16)


def _gdn_prep(qkv, proj, gpar, b, s, hps):
    nt = s // TILE
    hs = GDN_HEADS
    wid = hps * LANES
    ng = hs // hps

    def main(col0):
        return pl.BlockSpec((TILE, wid), lambda bi, i, h: (bi * nt + i, col0 + h))

    qc, kc, vc = 0, ng, 2 * ng
    seq = jax.ShapeDtypeStruct((b, s, GDN_WIDTH), BF16)
    seq_spec = pl.BlockSpec((1, TILE, wid), lambda bi, i, h: (bi, i, h))
    nchunk = TILE // CHUNK
    egl = jax.ShapeDtypeStruct((b * nt, nchunk, 2 * hs, LANES), F32)
    egl_spec = pl.BlockSpec((1, nchunk, 2 * hs, LANES), lambda bi, i, h: (bi * nt + i, 0, 0, 0))
    return pl.pallas_call(
        _gdn_prep_kernel,
        out_shape=(seq,) * 10 + (egl,),
        grid=(b, nt, ng),
        in_specs=[main(qc), main(kc), main(vc),
                  pl.BlockSpec((TILE, LANES), lambda bi, i, h: (bi * nt + i, COL_GATE // LANES)),
                  pl.BlockSpec((SUBLANES, LANES), lambda bi, i, h: (0, 0))],
        out_specs=(seq_spec,) * 10 + (egl_spec,),
        scratch_shapes=[pltpu.VMEM((TILE, LANES), F32), pltpu.VMEM((TILE, LANES), F32),
                        pltpu.VMEM((TILE, LANES), F32), pltpu.VMEM((LANES, 2 * TILE), F32),
                        pltpu.VMEM((LANES, 2 * TILE), F32)],
        compiler_params=_cparams("parallel", "parallel", "arbitrary"),
        name="gdn_prep",
    )(qkv, qkv, qkv, proj, gpar)


def _gdn_scan_kernel(uf_ref, wf_ref, qdf_ref, kdf_ref, atf_ref, eglf_ref,
                     ub_ref, wb_ref, qdb_ref, kdb_ref, atb_ref, eglb_ref,
                     of_ref, ob_ref, st_ref):
    @pl.when(pl.program_id(1) == 0)
    def _():
        st_ref[...] = jnp.zeros_like(st_ref)

    nchunk = TILE // CHUNK
    tn_dims = (((0,), (0,)), ((), ()))

    fwd_refs = (uf_ref, wf_ref, qdf_ref, kdf_ref, atf_ref, eglf_ref, of_ref)
    bwd_refs = (ub_ref, wb_ref, qdb_ref, kdb_ref, atb_ref, eglb_ref, ob_ref)

    def step(c, carry):
        chains = []
        for hd in range(GDN_HEADS):
            chains.append((hd, c, hd, fwd_refs))
            chains.append((GDN_HEADS + hd, nchunk - 1 - c, hd, bwd_refs))
        stage1 = []
        for idx, cc, hd, (u_ref, w_ref, qd_ref, kd_ref, at_ref, egl_ref, o_ref) in chains:
            rows = pl.ds(pl.multiple_of(cc * CHUNK, CHUNK), CHUNK)
            cols = slice(hd * LANES, (hd + 1) * LANES)
            sb = st_ref[idx].astype(BF16)
            ws = jnp.dot(w_ref[0, rows, cols], sb, preferred_element_type=F32)
            qs = jnp.dot(qd_ref[0, rows, cols], sb, preferred_element_type=F32)
            stage1.append((rows, cols, ws, qs))
        for (idx, cc, hd, refs), (rows, cols, ws, qs) in zip(chains, stage1):
            u_ref, w_ref, qd_ref, kd_ref, at_ref, egl_ref, o_ref = refs
            v_new = (u_ref[0, rows, cols].astype(F32) - ws).astype(BF16)
            v_pair = jnp.concatenate([v_new, v_new], axis=0)
            o_ref[0, rows, cols] = (qs + jnp.dot(at_ref[0, rows, cols], v_pair,
                                                 preferred_element_type=F32)).astype(o_ref.dtype)
            decay = egl_ref[0, pl.ds(cc, 1), idx, :]
            upd = lax.dot_general(kd_ref[0, rows, cols], v_new, tn_dims,
                                  preferred_element_type=F32)
            st_ref[idx] = st_ref[idx] * decay + upd
        return carry

    lax.fori_loop(0, nchunk, step, 0)


def _gdn_scan(prep, b, s):
    uf, wf, qdf, kdf, atf, ub, wb, qdb, kdb, atb, egl = prep
    nt = s // TILE
    hs = GDN_HEADS
    nchunk = TILE // CHUNK
    fwd = pl.BlockSpec((1, TILE, GDN_WIDTH), lambda bi, i: (bi, i, 0))
    bwd = pl.BlockSpec((1, TILE, GDN_WIDTH), lambda bi, i: (bi, nt - 1 - i, 0))
    egl_f = pl.BlockSpec((1, nchunk, 2 * hs, LANES), lambda bi, i: (bi * nt + i, 0, 0, 0))
    egl_b = pl.BlockSpec((1, nchunk, 2 * hs, LANES), lambda bi, i: (bi * nt + nt - 1 - i, 0, 0, 0))
    out = jax.ShapeDtypeStruct((b, s, GDN_WIDTH), BF16)
    return pl.pallas_call(
        _gdn_scan_kernel,
        out_shape=(out, out),
        grid=(b, nt),
        in_specs=[fwd] * 5 + [egl_f] + [bwd] * 5 + [egl_b],
        out_specs=(fwd, bwd),
        scratch_shapes=[pltpu.VMEM((2 * hs, GDN_DK, GDN_DV), F32)],
        compiler_params=_cparams("parallel", "arbitrary"),
        name="gdn_scan",
    )(uf, wf, qdf, kdf, atf, egl, ub, wb, qdb, kdb, atb, egl)


def _outproj_kernel(of_ref, ob_ref, za_ref, mixb_ref, x_ref, gn_ref, gp_ref, w_ref, y_ref):
    o = of_ref[...].astype(F32) + ob_ref[...].astype(F32)
    za = za_ref[...]
    parts = []
    for h in range(GDN_HEADS):
        sl = slice(h * GDN_DV, (h + 1) * GDN_DV)
        parts.append((_rms(o[:, sl], gn_ref[...]) * _silu(za[:, sl])).astype(BF16))
    mix_a = jnp.concatenate(parts, axis=1)
    y = jnp.dot(mix_a, w_ref[0:GDN_WIDTH, :], preferred_element_type=F32)
    y = y + jnp.dot(mixb_ref[...], w_ref[GDN_WIDTH:, :], preferred_element_type=F32)
    y_ref[...] = x_ref[...] + _rms(y, gp_ref[...])


def _outproj(o_f, o_b, proj, mix_b, x2d, gn, gp, w_out, tm):
    t = x2d.shape[0]
    row = lambda i: (i, 0)
    const = lambda i: (0, 0)
    return pl.pallas_call(
        _outproj_kernel,
        out_shape=jax.ShapeDtypeStruct((t, D_MODEL), F32),
        grid=(t // tm,),
        in_specs=[
            pl.BlockSpec((tm, GDN_WIDTH), row),
            pl.BlockSpec((tm, GDN_WIDTH), row),
            pl.BlockSpec((tm, GDN_WIDTH), lambda i: (i, COL_ZA // GDN_WIDTH)),
            pl.BlockSpec((tm, MLA_WIDTH), row),
            pl.BlockSpec((tm, D_MODEL), row),
            pl.BlockSpec((1, GDN_DV), const),
            pl.BlockSpec((1, D_MODEL), const),
            pl.BlockSpec((D_MIX, D_MODEL), const),
        ],
        out_specs=pl.BlockSpec((tm, D_MODEL), row),
        compiler_params=_cparams("parallel"),
        name="outproj",
    )(o_f, o_b, proj, mix_b, x2d, gn, gp, w_out)


def _swap_halves(w):
    half = MLA_ROPE // 2
    return jnp.concatenate([w[..., half:], w[..., :half]], axis=-1)


def _prep_layer(w_in, mla_w_uq, mla_w_ukv, w_out, a_log, dt_bias):
    o = np.cumsum((0, GDN_QKV, GDN_WIDTH, 2 * GDN_HEADS, 2 * GDN_HEADS, Q_LORA, KV_LORA,
                   MLA_ROPE, MLA_WIDTH))
    qkv, za, bl, al, cq, ckv, kpe, zb = (w_in[:, o[n]:o[n + 1]] for n in range(8))
    pad = jnp.zeros((D_MODEL, LANES - 4 * GDN_HEADS), w_in.dtype)
    w_pad = jnp.concatenate([qkv, za, zb, cq, ckv, kpe, _swap_halves(kpe), bl, al, pad],
                            axis=1).astype(BF16)
    wq = mla_w_uq.reshape(Q_LORA, MLA_HEADS, MLA_NOPE + MLA_ROPE)
    wq_ext = jnp.concatenate([wq, _swap_halves(wq[..., MLA_NOPE:])], axis=-1)
    wq_ext = wq_ext.reshape(Q_LORA, MLA_HEADS * QK_PAD).astype(BF16)
    wkv = mla_w_ukv.astype(BF16)
    gpar = jnp.zeros((SUBLANES, LANES), F32)
    n_gate = 2 * GDN_HEADS
    gpar = gpar.at[0, n_gate:2 * n_gate].set(a_log.reshape(-1))
    gpar = gpar.at[1, n_gate:2 * n_gate].set(dt_bias.reshape(-1))
    return w_pad, wq_ext, wkv, w_out.astype(BF16), gpar


def _rope_tables(s):
    pos = jnp.arange(s, dtype=F32)
    inv = ROPE_BASE ** (-jnp.arange(0, MLA_ROPE, 2, dtype=F32) / MLA_ROPE)
    ang = pos[:, None] * inv[None, :]
    cos, sin = jnp.cos(ang), jnp.sin(ang)
    zero = jnp.zeros((s, LANES - MLA_ROPE), F32)
    return (jnp.concatenate([cos, cos, zero], axis=1),
            jnp.concatenate([-sin, sin, zero], axis=1))


def _pick(n, prefs):
    for p in prefs:
        if n % p == 0:
            return p
    return n


def _tiles(b, s):
    t = b * s
    return dict(tm_in=_pick(s, (1024, 512, 256)), tn_in=GDN_WIDTH // 2,
                ts_mla=_pick(s, (512, 256)),
                tq=_pick(s, (512, 256, 128)), tk=_pick(s // 2, (1024, 512, 256, 128)),
                tm_out=_pick(t, (512, 256)), gdn_hps=4)


def _layer(x2d, b, s, tl, pre_g, post_g, conv_w, gdn_norm_g, q_norm_g, kv_norm_g,
           w_pad, wq_ext, wkv, w_out, gpar, cos2, sin2):
    qkv, proj = _inproj(x2d, pre_g.reshape(1, -1), w_pad, conv_w, s, tl["tm_in"], tl["tn_in"])
    prep = _gdn_prep(qkv, proj, gpar, b, s, tl["gdn_hps"])
    o_f, o_b = _gdn_scan(prep, b, s)
    q, k, v = _mla_proj(proj, q_norm_g.reshape(1, -1), kv_norm_g.reshape(1, -1), wq_ext, wkv,
                        cos2, sin2, b, s, tl["ts_mla"])
    mix_b = _attention(q, k, v, proj, b, s, tl["tq"], tl["tk"])
    return _outproj(o_f.reshape(b * s, -1), o_b.reshape(b * s, -1), proj,
                    mix_b.reshape(b * s, -1), x2d, gdn_norm_g.reshape(1, -1),
                    post_g.reshape(1, -1), w_out, tl["tm_out"])


def _trunk(x, layers, pre_norm_g, post_norm_g, conv_w, gdn_norm_g, mla_q_norm_g, mla_kv_norm_g):
    b, s, d = x.shape
    assert d == D_MODEL and s % TILE == 0
    tl = _tiles(b, s)
    cos2, sin2 = _rope_tables(s)
    x2d = x.reshape(b * s, d)
    for l in range(DEPTH):
        x2d = _layer(x2d, b, s, tl, pre_norm_g[l], post_norm_g[l], conv_w[l], gdn_norm_g[l],
                     mla_q_norm_g[l], mla_kv_norm_g[l], *layers[l], cos2, sin2)
    return x2d.reshape(b, s, d)


def kernel(x_prompt, x_sample, pre_norm_g, post_norm_g, w_in, conv_w, gdn_a_log, gdn_dt_bias,
           gdn_norm_g, mla_q_norm_g, mla_kv_norm_g, mla_w_uq, mla_w_ukv, w_out):
    layers = [_prep_layer(w_in[l], mla_w_uq[l], mla_w_ukv[l], w_out[l], gdn_a_log[l],
                          gdn_dt_bias[l]) for l in range(DEPTH)]
    args = (layers, pre_norm_g, post_norm_g, conv_w, gdn_norm_g, mla_q_norm_g, mla_kv_norm_g)
    return (_trunk(x_prompt, *args), _trunk(x_sample, *args))
```

```python
import functools

import numpy as np
import jax
import jax.numpy as jnp
from jax import lax
from jax.experimental import pallas as pl
from jax.experimental.pallas import tpu as pltpu

F32 = jnp.float32
BF16 = jnp.bfloat16

D_MODEL = 2048
DEPTH = 2
GDN_HEADS = 8
GDN_DK = 128
GDN_DV = 128
GDN_WIDTH = GDN_HEADS * GDN_DV
GDN_QKV = 2 * GDN_HEADS * GDN_DK + GDN_WIDTH
CONV_K = 5
CHUNK = 64
MLA_HEADS = 8
MLA_NOPE = 128
MLA_ROPE = 64
MLA_DV = 128
MLA_WIDTH = MLA_HEADS * MLA_DV
Q_LORA = 512
KV_LORA = 256
ROPE_BASE = 10000.0
D_MIX = GDN_WIDTH + MLA_WIDTH
EPS = 1e-6
MLA_SCALE = (MLA_NOPE + MLA_ROPE) ** -0.5
GDN_SCALE = GDN_DK ** -0.5
LOG2_E = float(np.log2(np.e))

LANES = 128
SUBLANES = 8
VMEM_LIMIT = 56 * 1024 * 1024

COL_QKV = 0
COL_ZA = COL_QKV + GDN_QKV
COL_ZB = COL_ZA + GDN_WIDTH
COL_CQ = COL_ZB + MLA_WIDTH
COL_CKV = COL_CQ + Q_LORA
COL_KPE = COL_CKV + KV_LORA
COL_GATE = COL_KPE + 2 * MLA_ROPE
D_PROJ = COL_GATE + LANES
MISC_W = D_PROJ - COL_CQ
QK_PAD = 2 * LANES
V_PAD = 2 * LANES
QK_SCALE_LOG2 = MLA_SCALE * LOG2_E
TILE = 4 * CHUNK
CHUNK_LOG2 = CHUNK.bit_length() - 1
assert 1 << CHUNK_LOG2 == CHUNK and 2 * CHUNK == LANES

assert D_PROJ == 6144 and MISC_W == 1024 and COL_GATE - COL_CQ == 896


def _cparams(*sem):
    return pltpu.CompilerParams(dimension_semantics=sem, vmem_limit_bytes=VMEM_LIMIT)


def _silu(z):
    return z / (1.0 + jnp.exp(-z))


def _rms(x, g):
    return x * lax.rsqrt(jnp.mean(x * x, axis=-1, keepdims=True) + EPS) * g


def _inproj_kernel(x_ref, g_ref, w_ref, o_ref, h_ref):
    @pl.when(pl.program_id(1) == 0)
    def _():
        h_ref[...] = _rms(x_ref[...], g_ref[...]).astype(BF16)

    o_ref[...] = jnp.dot(h_ref[...], w_ref[...], preferred_element_type=F32)


def _inproj(x2d, g, w_pad, tm, tn):
    t = x2d.shape[0]
    return pl.pallas_call(
        _inproj_kernel,
        out_shape=jax.ShapeDtypeStruct((t, D_PROJ), F32),
        grid=(t // tm, D_PROJ // tn),
        in_specs=[
            pl.BlockSpec((tm, D_MODEL), lambda i, j: (i, 0)),
            pl.BlockSpec((1, D_MODEL), lambda i, j: (0, 0)),
            pl.BlockSpec((D_MODEL, tn), lambda i, j: (0, j)),
        ],
        out_specs=pl.BlockSpec((tm, tn), lambda i, j: (i, j)),
        scratch_shapes=[pltpu.VMEM((tm, D_MODEL), BF16)],
        compiler_params=_cparams("parallel", "arbitrary"),
        name="inproj",
    )(x2d, g, w_pad)


def _mla_proj_kernel(p_ref, gq_ref, gkv_ref, wq_ref, wkv_ref, cos_ref, sin_ref,
                     q_ref, k_ref, v_ref):
    t = p_ref[...]
    cqn = _rms(t[:, 0:Q_LORA], gq_ref[...]).astype(BF16)
    ckn = _rms(t[:, Q_LORA:Q_LORA + KV_LORA], gkv_ref[...]).astype(BF16)
    kpe2 = t[:, COL_KPE - COL_CQ:COL_GATE - COL_CQ]
    qe = jnp.dot(cqn, wq_ref[...], preferred_element_type=F32)
    kve = jnp.dot(ckn, wkv_ref[...], preferred_element_type=F32)
    c2 = cos_ref[...]
    s2 = sin_ref[...]
    krope = (kpe2 * c2 + pltpu.roll(kpe2, MLA_ROPE, axis=1) * s2).astype(BF16)
    ones = jnp.ones((t.shape[0], V_PAD - MLA_DV), BF16)
    for h in range(MLA_HEADS):
        qn = qe[:, h * QK_PAD:h * QK_PAD + LANES]
        qp = qe[:, h * QK_PAD + LANES:(h + 1) * QK_PAD]
        qr = qp * c2 + pltpu.roll(qp, MLA_ROPE, axis=1) * s2
        q_ref[0, h, :, 0:LANES] = (qn * QK_SCALE_LOG2).astype(BF16)
        q_ref[0, h, :, LANES:QK_PAD] = (qr * QK_SCALE_LOG2).astype(BF16)
        k_ref[0, h, :, 0:LANES] = kve[:, h * QK_PAD:h * QK_PAD + LANES].astype(BF16)
        k_ref[0, h, :, LANES:QK_PAD] = krope
        v_ref[0, h, :, 0:MLA_DV] = kve[:, h * QK_PAD + LANES:(h + 1) * QK_PAD].astype(BF16)
        v_ref[0, h, :, MLA_DV:] = ones


def _mla_proj(proj, gq, gkv, wq_ext, wkv, cos2, sin2, b, s, ts):
    nt = s // ts
    hs = MLA_HEADS
    return pl.pallas_call(
        _mla_proj_kernel,
        out_shape=(jax.ShapeDtypeStruct((b, hs, s, QK_PAD), BF16),
                   jax.ShapeDtypeStruct((b, hs, s, QK_PAD), BF16),
                   jax.ShapeDtypeStruct((b, hs, s, V_PAD), BF16)),
        grid=(b, nt),
        in_specs=[
            pl.BlockSpec((ts, MISC_W), lambda bi, i: (bi * nt + i, COL_CQ // MISC_W)),
            pl.BlockSpec((1, Q_LORA), lambda bi, i: (0, 0)),
            pl.BlockSpec((1, KV_LORA), lambda bi, i: (0, 0)),
            pl.BlockSpec((Q_LORA, hs * QK_PAD), lambda bi, i: (0, 0)),
            pl.BlockSpec((KV_LORA, hs * QK_PAD), lambda bi, i: (0, 0)),
            pl.BlockSpec((ts, LANES), lambda bi, i: (i, 0)),
            pl.BlockSpec((ts, LANES), lambda bi, i: (i, 0)),
        ],
        out_specs=(pl.BlockSpec((1, hs, ts, QK_PAD), lambda bi, i: (bi, 0, i, 0)),
                   pl.BlockSpec((1, hs, ts, QK_PAD), lambda bi, i: (bi, 0, i, 0)),
                   pl.BlockSpec((1, hs, ts, V_PAD), lambda bi, i: (bi, 0, i, 0))),
        compiler_params=_cparams("parallel", "parallel"),
        name="mla_proj",
    )(proj, gq, gkv, wq_ext, wkv, cos2, sin2)


def _attn_kernel(q_ref, k_ref, v_ref, z_ref, o_ref, sa_ref, sb_ref, p_ref, m_ref, acc_ref,
                 *, tq, tk, rb):
    s_len = k_ref.shape[2]
    nq, nk = s_len // tq, s_len // tk
    nt_dims = (((1,), (1,)), ((), ()))
    blocks = [slice(r * rb, (r + 1) * rb) for r in range(tq // rb)]
    lane_tiles = [slice(t * LANES, (t + 1) * LANES) for t in range(tk // LANES)]

    def scores(i, j, s_ref):
        q = q_ref[0, 0, pl.ds(pl.multiple_of(i * tq, tq), tq), :]
        s_ref[...] = lax.dot_general(q, k_ref[0, 0, j * tk:(j + 1) * tk, :], nt_dims,
                                     preferred_element_type=F32)

    def softmax_and_pv(j, s_ref):
        mx_parts = []
        for rows in blocks:
            mx = s_ref[rows, lane_tiles[0]]
            for lt in lane_tiles[1:]:
                mx = jnp.maximum(mx, s_ref[rows, lt])
            mx_parts.append(mx)
        mx_all = jnp.concatenate(mx_parts, axis=0)
        row_max = jnp.broadcast_to(jnp.max(mx_all, axis=-1, keepdims=True), mx_all.shape)
        if j == 0:
            m_new = row_max
        else:
            m_old = m_ref[...]
            m_new = jnp.maximum(m_old, row_max)
            alpha = jnp.exp2(m_old - m_new)
        m_ref[...] = m_new
        for rows in blocks:
            m_b = m_new[rows]
            for lt in lane_tiles:
                p_ref[rows, lt] = jnp.exp2(s_ref[rows, lt] - m_b).astype(BF16)
        pv = jnp.dot(p_ref[...], v_ref[0, 0, j * tk:(j + 1) * tk, :],
                     preferred_element_type=F32)
        if j == 0:
            acc_ref[...] = pv
        else:
            acc_ref[...] = acc_ref[...] * jnp.concatenate([alpha, alpha], axis=1) + pv

    bufs = (sa_ref, sb_ref)
    scores(0, 0, sa_ref)

    def body(i, carry):
        for j in range(nk):
            if j + 1 < nk:
                scores(i, j + 1, bufs[(j + 1) % 2])
            else:
                scores(jnp.minimum(i + 1, nq - 1), 0, bufs[0])
            softmax_and_pv(j, bufs[j % 2])
        rows = pl.ds(pl.multiple_of(i * tq, tq), tq)
        acc = acc_ref[...]
        o_ref[0, rows, :] = ((acc[:, 0:MLA_DV] / acc[:, MLA_DV:])
                             * _silu(z_ref[rows, :])).astype(BF16)
        return carry

    lax.fori_loop(0, nq, body, 0)


def _attention(q, k, v, proj, b, s, tq, tk):
    hs = MLA_HEADS
    zb0 = COL_ZB // LANES
    assert (s // tk) % 2 == 0
    return pl.pallas_call(
        functools.partial(_attn_kernel, tq=tq, tk=tk, rb=2 * SUBLANES),
        out_shape=jax.ShapeDtypeStruct((b, s, MLA_WIDTH), BF16),
        grid=(b, hs),
        in_specs=[
            pl.BlockSpec((1, 1, s, QK_PAD), lambda bi, h: (bi, h, 0, 0)),
            pl.BlockSpec((1, 1, s, QK_PAD), lambda bi, h: (bi, h, 0, 0)),
            pl.BlockSpec((1, 1, s, V_PAD), lambda bi, h: (bi, h, 0, 0)),
            pl.BlockSpec((s, LANES), lambda bi, h: (bi, zb0 + h)),
        ],
        out_specs=pl.BlockSpec((1, s, MLA_DV), lambda bi, h: (bi, 0, h)),
        scratch_shapes=[pltpu.VMEM((tq, tk), F32), pltpu.VMEM((tq, tk), F32),
                        pltpu.VMEM((tq, tk), BF16),
                        pltpu.VMEM((tq, LANES), F32), pltpu.VMEM((tq, V_PAD), F32)],
        compiler_params=_cparams("parallel", "arbitrary"),
        name="attention",
    )(q, k, v, proj)


def _tile_masks():
    r = lax.broadcasted_iota(jnp.int32, (TILE, TILE), 0)
    c = lax.broadcasted_iota(jnp.int32, (TILE, TILE), 1)
    same = (r >> CHUNK_LOG2) == (c >> CHUNK_LOG2)
    return (same & (r >= c)).astype(F32), (same & (r <= c)).astype(F32)


def _col(x, lane):
    li = lax.broadcasted_iota(jnp.int32, x.shape, 1)
    col = jnp.sum(jnp.where(li == lane, x, 0.0), axis=1, keepdims=True)
    return jnp.broadcast_to(col, x.shape)


def _dup_chunks_t(cs):
    parts = []
    for c in range(TILE // CHUNK):
        blk = cs[c * CHUNK:(c + 1) * CHUNK, :]
        parts += [blk, blk]
    return jnp.concatenate(parts, axis=0).T


def _gdn_prep_kernel(q_ref, qp_ref, qn_ref, k_ref, kp_ref, kn_ref, v_ref, vp_ref, vn_ref,
                     cwq_ref, cwk_ref, cwv_ref, gate_ref, gpar_ref,
                     uf_ref, wf_ref, qdf_ref, kdf_ref, atf_ref,
                     ub_ref, wb_ref, qdb_ref, kdb_ref, atb_ref, egl_ref,
                     xq_ref, xk_ref, xv_ref, g_ref, csl_ref, csu_ref, cslt_ref, csut_ref):
    i = pl.program_id(1)
    nt = pl.num_programs(1)
    hg = pl.program_id(2)
    hps = q_ref.shape[1] // LANES

    @pl.when(hg == 0)
    def _():
        t = gate_ref[...]
        a_log = gpar_ref[0:1, :]
        dt_b = gpar_ref[1:2, :]
        li = lax.broadcasted_iota(jnp.int32, t.shape, 1)
        xg = t + dt_b
        sp = jnp.maximum(xg, 0.0) + jnp.log1p(jnp.exp(-jnp.abs(xg)))
        gdec = -jnp.exp(a_log) * sp
        beta = 1.0 / (1.0 + jnp.exp(-t))
        n_gate = 2 * GDN_HEADS
        g = jnp.where(li < n_gate, beta, jnp.where(li < 2 * n_gate, gdec, 0.0))
        g_ref[...] = g
        tri_l, tri_u = _tile_masks()
        hi = g.astype(BF16)
        r1 = g - hi.astype(F32)
        mid = r1.astype(BF16)
        lo = (r1 - mid.astype(F32)).astype(BF16)
        pieces = jnp.concatenate([hi, mid, lo], axis=1)

        def cumsum(tri):
            c3 = jnp.dot(tri.astype(BF16), pieces, preferred_element_type=F32)
            return c3[:, 0:LANES] + c3[:, LANES:2 * LANES] + c3[:, 2 * LANES:]

        csl = cumsum(tri_l)
        csu = cumsum(tri_u)
        csl_ref[...] = csl
        csu_ref[...] = csu
        cslt_ref[...] = _dup_chunks_t(csl)
        csut_ref[...] = _dup_chunks_t(csu)

    def fill_halo(xs_ref, main_ref, prev_ref, next_ref):
        xs_ref[0:SUBLANES, :] = jnp.where(i > 0, prev_ref[...], 0.0)
        xs_ref[SUBLANES:SUBLANES + TILE, :] = main_ref[...]
        xs_ref[SUBLANES + TILE:, :] = jnp.where(i < nt - 1, next_ref[...], 0.0)

    fill_halo(xq_ref, q_ref, qp_ref, qn_ref)
    fill_halo(xk_ref, k_ref, kp_ref, kn_ref)
    fill_halo(xv_ref, v_ref, vp_ref, vn_ref)

    def conv_silu(xs_ref, w_ref, cols):
        pad = (CONV_K - 1) // 2
        xs = xs_ref[:, cols]
        n = xs.shape[0]
        acc = None
        for tap in range(CONV_K):
            sh = xs if tap == pad else pltpu.roll(xs, (pad - tap) % n, axis=0)
            term = sh[SUBLANES:SUBLANES + TILE, :] * w_ref[tap:tap + 1, cols]
            acc = term if acc is None else acc + term
        return _silu(acc)

    def l2n(x):
        return x * lax.rsqrt(jnp.sum(x * x, axis=-1, keepdims=True) + EPS)

    ri = lax.broadcasted_iota(jnp.int32, (CHUNK, LANES), 0)
    li = lax.broadcasted_iota(jnp.int32, (CHUNK, LANES), 1)
    lj = li & (CHUNK - 1)
    left = li < CHUNK
    left_f = left.astype(F32)
    eye_left = (ri == li).astype(F32)
    zeros_r = jnp.zeros((CHUNK, 2 * LANES), BF16)
    nt_dims = (((1,), (1,)), ((), ()))
    nchunk = TILE // CHUNK

    g = g_ref[...]
    csl = csl_ref[...]
    csu = csu_ref[...]
    nh = GDN_HEADS
    lower = (ri >= lj).astype(F32)
    slower = (ri > lj).astype(F32)
    upper = (ri <= lj).astype(F32)
    supper = (ri < lj).astype(F32)

    def load_head(hh):
        cols = slice(hh * LANES, (hh + 1) * LANES)
        q = l2n(conv_silu(xq_ref, cwq_ref, cols)) * GDN_SCALE
        k = l2n(conv_silu(xk_ref, cwk_ref, cols))
        v = conv_silu(xv_ref, cwv_ref, cols)
        return hh, q, k, v

    def setup_head(loaded):
        hh, q, k, v = loaded
        h = hg * hps + hh
        cols = slice(hh * LANES, (hh + 1) * LANES)
        qb = q.astype(BF16)
        kb16 = k.astype(BF16)
        grams = []
        for c in range(nchunk):
            rows = slice(c * CHUNK, (c + 1) * CHUNK)
            k2 = jnp.concatenate([kb16[rows], kb16[rows]], axis=0)
            grams.append((lax.dot_general(kb16[rows], k2, nt_dims, preferred_element_type=F32),
                          lax.dot_general(qb[rows], k2, nt_dims, preferred_element_type=F32)))

        def setup_direction(beta_lane, g_lane, cs, cs_other, cst_ref, tri, stri,
                            u_ref, w_ref, qd_ref, kd_ref, at_ref, egl_row):
            beta = _col(g, beta_lane)
            gc = _col(cs, g_lane)
            rest = _col(cs_other - g, g_lane)
            egc = jnp.exp(gc)
            qd_ref[0, :, cols] = (q * egc).astype(BF16)
            kd_ref[0, :, cols] = (k * jnp.exp(rest)).astype(BF16)
            vb = (v * beta).astype(BF16)
            kbg = (k * (beta * egc)).astype(BF16)
            tot = jnp.exp(gc + rest)
            gc_rows = cst_ref[pl.ds(g_lane, 1), :]
            chains = []
            for c in range(nchunk):
                rows = slice(c * CHUNK, (c + 1) * CHUNK)
                kk2, qk2 = grams[c]
                gc_row = gc_rows[:, c * LANES:(c + 1) * LANES]
                decay = jnp.exp(jnp.minimum(gc[rows] - gc_row, 0.0))
                x = jnp.where(left, eye_left, -(beta[rows] * kk2 * decay * stri))
                half = left_f if c % 2 == 0 else 1.0 - left_f
                at_ref[0, rows, cols] = (qk2 * decay * (tri * half)).astype(BF16)
                egl_ref[0, c, pl.ds(egl_row, 1), :] = tot[c * CHUNK:c * CHUNK + 1, :]
                rhs = jnp.concatenate(
                    [jnp.concatenate([vb[rows], kbg[rows]], axis=1), zeros_r], axis=0)
                chains.append((x, rhs, rows, cols, u_ref, w_ref))
            return chains

        return (setup_direction(h, 2 * nh + h, csl, csu, cslt_ref, lower, slower,
                                uf_ref, wf_ref, qdf_ref, kdf_ref, atf_ref, h)
                + setup_direction(nh + h, 3 * nh + h, csu, csl, csut_ref, upper, supper,
                                  ub_ref, wb_ref, qdb_ref, kdb_ref, atb_ref, nh + h))

    eye_w = eye_left.astype(BF16)

    def invert_and_apply(chains):
        xs = [ch[0].astype(BF16) for ch in chains]
        for _ in range(CHUNK_LOG2):
            xs = [jnp.dot(xb, jnp.concatenate([eye_w, xb], axis=0),
                          preferred_element_type=F32).astype(BF16) for xb in xs]
        for x, (_, rhs, rows, cols, u_ref, w_ref) in zip(xs, chains):
            uw = jnp.dot(x, rhs, preferred_element_type=F32)
            u_ref[0, rows, cols] = uw[:, 0:GDN_DV].astype(BF16)
            w_ref[0, rows, cols] = uw[:, GDN_DV:].astype(BF16)

    group = 2
    pending = None
    for g0 in range(0, hps, group):
        loaded = [load_head(hh) for hh in range(g0, min(g0 + group, hps))]
        if pending is not None:
            invert_and_apply(pending)
        pending = []
        for ld in loaded:
            pending += setup_head(ld)
    invert_and_apply(pending)


def _gdn_prep(proj, conv_w, gpar, b, s, hps):
    nt = s // TILE
    hs = GDN_HEADS
    rows8 = TILE // SUBLANES
    last8 = b * s // SUBLANES - 1
    wid = hps * LANES
    ng = hs // hps

    def main(col0):
        return pl.BlockSpec((TILE, wid), lambda bi, i, h: (bi * nt + i, col0 + h))

    def prev(col0):
        return pl.BlockSpec((SUBLANES, wid),
                            lambda bi, i, h: (jnp.maximum((bi * nt + i) * rows8 - 1, 0), col0 + h))

    def nxt(col0):
        return pl.BlockSpec((SUBLANES, wid),
                            lambda bi, i, h: (jnp.minimum((bi * nt + i + 1) * rows8, last8), col0 + h))

    def cw(col0):
        return pl.BlockSpec((CONV_K, wid), lambda bi, i, h: (0, col0 + h))

    qc, kc, vc = 0, ng, 2 * ng
    seq = jax.ShapeDtypeStruct((b, s, GDN_WIDTH), BF16)
    seq_spec = pl.BlockSpec((1, TILE, wid), lambda bi, i, h: (bi, i, h))
    nchunk = TILE // CHUNK
    egl = jax.ShapeDtypeStruct((b * nt, nchunk, 2 * hs, LANES), F32)
    egl_spec = pl.BlockSpec((1, nchunk, 2 * hs, LANES), lambda bi, i, h: (bi * nt + i, 0, 0, 0))
    return pl.pallas_call(
        _gdn_prep_kernel,
        out_shape=(seq,) * 10 + (egl,),
        grid=(b, nt, ng),
        in_specs=[main(qc), prev(qc), nxt(qc), main(kc), prev(kc), nxt(kc),
                  main(vc), prev(vc), nxt(vc), cw(qc), cw(kc), cw(vc),
                  pl.BlockSpec((TILE, LANES), lambda bi, i, h: (bi * nt + i, COL_GATE // LANES)),
                  pl.BlockSpec((SUBLANES, LANES), lambda bi, i, h: (0, 0))],
        out_specs=(seq_spec,) * 10 + (egl_spec,),
        scratch_shapes=[pltpu.VMEM((TILE + 2 * SUBLANES, wid), F32)] * 3 + [
                        pltpu.VMEM((TILE, LANES), F32), pltpu.VMEM((TILE, LANES), F32),
                        pltpu.VMEM((TILE, LANES), F32), pltpu.VMEM((LANES, 2 * TILE), F32),
                        pltpu.VMEM((LANES, 2 * TILE), F32)],
        compiler_params=_cparams("parallel", "parallel", "arbitrary"),
        name="gdn_prep",
    )(*([proj] * 9), conv_w, conv_w, conv_w, proj, gpar)


def _gdn_scan_kernel(uf_ref, wf_ref, qdf_ref, kdf_ref, atf_ref, eglf_ref,
                     ub_ref, wb_ref, qdb_ref, kdb_ref, atb_ref, eglb_ref,
                     of_ref, ob_ref, st_ref):
    @pl.when(pl.program_id(1) == 0)
    def _():
        st_ref[...] = jnp.zeros_like(st_ref)

    nchunk = TILE // CHUNK
    tn_dims = (((0,), (0,)), ((), ()))

    fwd_refs = (uf_ref, wf_ref, qdf_ref, kdf_ref, atf_ref, eglf_ref, of_ref)
    bwd_refs = (ub_ref, wb_ref, qdb_ref, kdb_ref, atb_ref, eglb_ref, ob_ref)

    def step(c, carry):
        chains = []
        for hd in range(GDN_HEADS):
            chains.append((hd, c, hd, fwd_refs))
            chains.append((GDN_HEADS + hd, nchunk - 1 - c, hd, bwd_refs))
        stage1 = []
        for idx, cc, hd, (u_ref, w_ref, qd_ref, kd_ref, at_ref, egl_ref, o_ref) in chains:
            rows = pl.ds(pl.multiple_of(cc * CHUNK, CHUNK), CHUNK)
            cols = slice(hd * LANES, (hd + 1) * LANES)
            sb = st_ref[idx].astype(BF16)
            ws = jnp.dot(w_ref[0, rows, cols], sb, preferred_element_type=F32)
            qs = jnp.dot(qd_ref[0, rows, cols], sb, preferred_element_type=F32)
            stage1.append((rows, cols, ws, qs))
        for (idx, cc, hd, refs), (rows, cols, ws, qs) in zip(chains, stage1):
            u_ref, w_ref, qd_ref, kd_ref, at_ref, egl_ref, o_ref = refs
            v_new = (u_ref[0, rows, cols].astype(F32) - ws).astype(BF16)
            v_pair = jnp.concatenate([v_new, v_new], axis=0)
            o_ref[0, rows, cols] = (qs + jnp.dot(at_ref[0, rows, cols], v_pair,
                                                 preferred_element_type=F32)).astype(o_ref.dtype)
            decay = egl_ref[0, pl.ds(cc, 1), idx, :]
            upd = lax.dot_general(kd_ref[0, rows, cols], v_new, tn_dims,
                                  preferred_element_type=F32)
            st_ref[idx] = st_ref[idx] * decay + upd
        return carry

    lax.fori_loop(0, nchunk, step, 0)


def _gdn_scan(prep, b, s):
    uf, wf, qdf, kdf, atf, ub, wb, qdb, kdb, atb, egl = prep
    nt = s // TILE
    hs = GDN_HEADS
    nchunk = TILE // CHUNK
    fwd = pl.BlockSpec((1, TILE, GDN_WIDTH), lambda bi, i: (bi, i, 0))
    bwd = pl.BlockSpec((1, TILE, GDN_WIDTH), lambda bi, i: (bi, nt - 1 - i, 0))
    egl_f = pl.BlockSpec((1, nchunk, 2 * hs, LANES), lambda bi, i: (bi * nt + i, 0, 0, 0))
    egl_b = pl.BlockSpec((1, nchunk, 2 * hs, LANES), lambda bi, i: (bi * nt + nt - 1 - i, 0, 0, 0))
    out = jax.ShapeDtypeStruct((b, s, GDN_WIDTH), BF16)
    return pl.pallas_call(
        _gdn_scan_kernel,
        out_shape=(out, out),
        grid=(b, nt),
        in_specs=[fwd] * 5 + [egl_f] + [bwd] * 5 + [egl_b],
        out_specs=(fwd, bwd),
        scratch_shapes=[pltpu.VMEM((2 * hs, GDN_DK, GDN_DV), F32)],
        compiler_params=_cparams("parallel", "arbitrary"),
        name="gdn_scan",
    )(uf, wf, qdf, kdf, atf, egl, ub, wb, qdb, kdb, atb, egl)


def _outproj_kernel(of_ref, ob_ref, za_ref, mixb_ref, x_ref, gn_ref, gp_ref, w_ref, y_ref):
    o = of_ref[...].astype(F32) + ob_ref[...].astype(F32)
    za = za_ref[...]
    parts = []
    for h in range(GDN_HEADS):
        sl = slice(h * GDN_DV, (h + 1) * GDN_DV)
        parts.append((_rms(o[:, sl], gn_ref[...]) * _silu(za[:, sl])).astype(BF16))
    mix_a = jnp.concatenate(parts, axis=1)
    y = jnp.dot(mix_a, w_ref[0:GDN_WIDTH, :], preferred_element_type=F32)
    y = y + jnp.dot(mixb_ref[...], w_ref[GDN_WIDTH:, :], preferred_element_type=F32)
    y_ref[...] = x_ref[...] + _rms(y, gp_ref[...])


def _outproj(o_f, o_b, proj, mix_b, x2d, gn, gp, w_out, tm):
    t = x2d.shape[0]
    row = lambda i: (i, 0)
    const = lambda i: (0, 0)
    return pl.pallas_call(
        _outproj_kernel,
        out_shape=jax.ShapeDtypeStruct((t, D_MODEL), F32),
        grid=(t // tm,),
        in_specs=[
            pl.BlockSpec((tm, GDN_WIDTH), row),
            pl.BlockSpec((tm, GDN_WIDTH), row),
            pl.BlockSpec((tm, GDN_WIDTH), lambda i: (i, COL_ZA // GDN_WIDTH)),
            pl.BlockSpec((tm, MLA_WIDTH), row),
            pl.BlockSpec((tm, D_MODEL), row),
            pl.BlockSpec((1, GDN_DV), const),
            pl.BlockSpec((1, D_MODEL), const),
            pl.BlockSpec((D_MIX, D_MODEL), const),
        ],
        out_specs=pl.BlockSpec((tm, D_MODEL), row),
        compiler_params=_cparams("parallel"),
        name="outproj",
    )(o_f, o_b, proj, mix_b, x2d, gn, gp, w_out)


def _swap_halves(w):
    half = MLA_ROPE // 2
    return jnp.concatenate([w[..., half:], w[..., :half]], axis=-1)


def _prep_layer(w_in, mla_w_uq, mla_w_ukv, w_out, a_log, dt_bias):
    o = np.cumsum((0, GDN_QKV, GDN_WIDTH, 2 * GDN_HEADS, 2 * GDN_HEADS, Q_LORA, KV_LORA,
                   MLA_ROPE, MLA_WIDTH))
    qkv, za, bl, al, cq, ckv, kpe, zb = (w_in[:, o[n]:o[n + 1]] for n in range(8))
    pad = jnp.zeros((D_MODEL, LANES - 4 * GDN_HEADS), w_in.dtype)
    w_pad = jnp.concatenate([qkv, za, zb, cq, ckv, kpe, _swap_halves(kpe), bl, al, pad],
                            axis=1).astype(BF16)
    wq = mla_w_uq.reshape(Q_LORA, MLA_HEADS, MLA_NOPE + MLA_ROPE)
    wq_ext = jnp.concatenate([wq, _swap_halves(wq[..., MLA_NOPE:])], axis=-1)
    wq_ext = wq_ext.reshape(Q_LORA, MLA_HEADS * QK_PAD).astype(BF16)
    wkv = mla_w_ukv.astype(BF16)
    gpar = jnp.zeros((SUBLANES, LANES), F32)
    n_gate = 2 * GDN_HEADS
    gpar = gpar.at[0, n_gate:2 * n_gate].set(a_log.reshape(-1))
    gpar = gpar.at[1, n_gate:2 * n_gate].set(dt_bias.reshape(-1))
    return w_pad, wq_ext, wkv, w_out.astype(BF16), gpar


def _rope_tables(s):
    pos = jnp.arange(s, dtype=F32)
    inv = ROPE_BASE ** (-jnp.arange(0, MLA_ROPE, 2, dtype=F32) / MLA_ROPE)
    ang = pos[:, None] * inv[None, :]
    cos, sin = jnp.cos(ang), jnp.sin(ang)
    zero = jnp.zeros((s, LANES - MLA_ROPE), F32)
    return (jnp.concatenate([cos, cos, zero], axis=1),
            jnp.concatenate([-sin, sin, zero], axis=1))


def _pick(n, prefs):
    for p in prefs:
        if n % p == 0:
            return p
    return n


def _tiles(b, s):
    t = b * s
    return dict(tm_in=_pick(t, (1024, 512, 256)), tn_in=1024,
                ts_mla=_pick(s, (512, 256)),
                tq=_pick(s, (512, 256, 128)), tk=_pick(s // 2, (1024, 512, 256, 128)),
                tm_out=_pick(t, (512, 256)), gdn_hps=8)


def _layer(x2d, b, s, tl, pre_g, post_g, conv_w, gdn_norm_g, q_norm_g, kv_norm_g,
           w_pad, wq_ext, wkv, w_out, gpar, cos2, sin2):
    proj = _inproj(x2d, pre_g.reshape(1, -1), w_pad, tl["tm_in"], tl["tn_in"])
    prep = _gdn_prep(proj, conv_w, gpar, b, s, tl["gdn_hps"])
    o_f, o_b = _gdn_scan(prep, b, s)
    q, k, v = _mla_proj(proj, q_norm_g.reshape(1, -1), kv_norm_g.reshape(1, -1), wq_ext, wkv,
                        cos2, sin2, b, s, tl["ts_mla"])
    mix_b = _attention(q, k, v, proj, b, s, tl["tq"], tl["tk"])
    return _outproj(o_f.reshape(b * s, -1), o_b.reshape(b * s, -1), proj,
                    mix_b.reshape(b * s, -1), x2d, gdn_norm_g.reshape(1, -1),
                    post_g.reshape(1, -1), w_out, tl["tm_out"])


def _trunk(x, layers, pre_norm_g, post_norm_g, conv_w, gdn_norm_g, mla_q_norm_g, mla_kv_norm_g):
    b, s, d = x.shape
    assert d == D_MODEL and s % TILE == 0
    tl = _tiles(b, s)
    cos2, sin2 = _rope_tables(s)
    x2d = x.reshape(b * s, d)
    for l in range(DEPTH):
        x2d = _layer(x2d, b, s, tl, pre_norm_g[l], post_norm_g[l], conv_w[l], gdn_norm_g[l],
                     mla_q_norm_g[l], mla_kv_norm_g[l], *layers[l], cos2, sin2)
    return x2d.reshape(b, s, d)


def kernel(x_prompt, x_sample, pre_norm_g, post_norm_g, w_in, conv_w, gdn_a_log, gdn_dt_bias,
           gdn_norm_g, mla_q_norm_g, mla_kv_norm_g, mla_w_uq, mla_w_ukv, w_out):
    layers = [_prep_layer(w_in[l], mla_w_uq[l], mla_w_ukv[l], w_out[l], gdn_a_log[l],
                          gdn_dt_bias[l]) for l in range(DEPTH)]
    args = (layers, pre_norm_g, post_norm_g, conv_w, gdn_norm_g, mla_q_norm_g, mla_kv_norm_g)
    return (_trunk(x_prompt, *args), _trunk(x_sample, *args))
```

```python
import functools

import numpy as np
import jax
import jax.numpy as jnp
from jax import lax
from jax.experimental import pallas as pl
from jax.experimental.pallas import tpu as pltpu

F32 = jnp.float32
BF16 = jnp.bfloat16

D_MODEL = 2048
DEPTH = 2
GDN_HEADS = 8
GDN_DK = 128
GDN_DV = 128
GDN_WIDTH = GDN_HEADS * GDN_DV
GDN_QKV = 2 * GDN_HEADS * GDN_DK + GDN_WIDTH
CONV_K = 5
CHUNK = 64
MLA_HEADS = 8
MLA_NOPE = 128
MLA_ROPE = 64
MLA_DV = 128
MLA_WIDTH = MLA_HEADS * MLA_DV
Q_LORA = 512
KV_LORA = 256
ROPE_BASE = 10000.0
D_MIX = GDN_WIDTH + MLA_WIDTH
EPS = 1e-6
MLA_SCALE = (MLA_NOPE + MLA_ROPE) ** -0.5
GDN_SCALE = GDN_DK ** -0.5
LOG2_E = float(np.log2(np.e))

LANES = 128
SUBLANES = 8
VMEM_LIMIT = 56 * 1024 * 1024

COL_QKV = 0
COL_ZA = COL_QKV + GDN_QKV
COL_ZB = COL_ZA + GDN_WIDTH
COL_CQ = COL_ZB + MLA_WIDTH
COL_CKV = COL_CQ + Q_LORA
COL_KPE = COL_CKV + KV_LORA
COL_GATE = COL_KPE + 2 * MLA_ROPE
D_PROJ = COL_GATE + LANES
MISC_W = D_PROJ - COL_CQ
QK_PAD = 2 * LANES
V_PAD = 2 * LANES
QK_SCALE_LOG2 = MLA_SCALE * LOG2_E
TILE = 4 * CHUNK
CHUNK_LOG2 = CHUNK.bit_length() - 1
assert 1 << CHUNK_LOG2 == CHUNK and 2 * CHUNK == LANES

assert D_PROJ == 6144 and MISC_W == 1024 and COL_GATE - COL_CQ == 896


def _cparams(*sem):
    return pltpu.CompilerParams(dimension_semantics=sem, vmem_limit_bytes=VMEM_LIMIT)


def _silu(z):
    return z / (1.0 + jnp.exp(-z))


def _rms(x, g):
    return x * lax.rsqrt(jnp.mean(x * x, axis=-1, keepdims=True) + EPS) * g


def _inproj_kernel(x_ref, g_ref, w_ref, o_ref, h_ref):
    @pl.when(pl.program_id(1) == 0)
    def _():
        h_ref[...] = _rms(x_ref[...], g_ref[...]).astype(BF16)

    o_ref[...] = jnp.dot(h_ref[...], w_ref[...], preferred_element_type=F32)


def _inproj(x2d, g, w_pad, tm, tn):
    t = x2d.shape[0]
    return pl.pallas_call(
        _inproj_kernel,
        out_shape=jax.ShapeDtypeStruct((t, D_PROJ), F32),
        grid=(t // tm, D_PROJ // tn),
        in_specs=[
            pl.BlockSpec((tm, D_MODEL), lambda i, j: (i, 0)),
            pl.BlockSpec((1, D_MODEL), lambda i, j: (0, 0)),
            pl.BlockSpec((D_MODEL, tn), lambda i, j: (0, j)),
        ],
        out_specs=pl.BlockSpec((tm, tn), lambda i, j: (i, j)),
        scratch_shapes=[pltpu.VMEM((tm, D_MODEL), BF16)],
        compiler_params=_cparams("parallel", "arbitrary"),
        name="inproj",
    )(x2d, g, w_pad)


def _mla_proj_kernel(p_ref, gq_ref, gkv_ref, wq_ref, wkv_ref, cos_ref, sin_ref,
                     q_ref, k_ref, v_ref):
    t = p_ref[...]
    cqn = _rms(t[:, 0:Q_LORA], gq_ref[...]).astype(BF16)
    ckn = _rms(t[:, Q_LORA:Q_LORA + KV_LORA], gkv_ref[...]).astype(BF16)
    kpe2 = t[:, COL_KPE - COL_CQ:COL_GATE - COL_CQ]
    qe = jnp.dot(cqn, wq_ref[...], preferred_element_type=F32)
    kve = jnp.dot(ckn, wkv_ref[...], preferred_element_type=F32)
    c2 = cos_ref[...]
    s2 = sin_ref[...]
    krope = (kpe2 * c2 + pltpu.roll(kpe2, MLA_ROPE, axis=1) * s2).astype(BF16)
    ones = jnp.ones((t.shape[0], V_PAD - MLA_DV), BF16)
    for h in range(MLA_HEADS):
        qn = qe[:, h * QK_PAD:h * QK_PAD + LANES]
        qp = qe[:, h * QK_PAD + LANES:(h + 1) * QK_PAD]
        qr = qp * c2 + pltpu.roll(qp, MLA_ROPE, axis=1) * s2
        q_ref[0, h, :, 0:LANES] = (qn * QK_SCALE_LOG2).astype(BF16)
        q_ref[0, h, :, LANES:QK_PAD] = (qr * QK_SCALE_LOG2).astype(BF16)
        k_ref[0, h, :, 0:LANES] = kve[:, h * QK_PAD:h * QK_PAD + LANES].astype(BF16)
        k_ref[0, h, :, LANES:QK_PAD] = krope
        v_ref[0, h, :, 0:MLA_DV] = kve[:, h * QK_PAD + LANES:(h + 1) * QK_PAD].astype(BF16)
        v_ref[0, h, :, MLA_DV:] = ones


def _mla_proj(proj, gq, gkv, wq_ext, wkv, cos2, sin2, b, s, ts):
    nt = s // ts
    hs = MLA_HEADS
    return pl.pallas_call(
        _mla_proj_kernel,
        out_shape=(jax.ShapeDtypeStruct((b, hs, s, QK_PAD), BF16),
                   jax.ShapeDtypeStruct((b, hs, s, QK_PAD), BF16),
                   jax.ShapeDtypeStruct((b, hs, s, V_PAD), BF16)),
        grid=(b, nt),
        in_specs=[
            pl.BlockSpec((ts, MISC_W), lambda bi, i: (bi * nt + i, COL_CQ // MISC_W)),
            pl.BlockSpec((1, Q_LORA), lambda bi, i: (0, 0)),
            pl.BlockSpec((1, KV_LORA), lambda bi, i: (0, 0)),
            pl.BlockSpec((Q_LORA, hs * QK_PAD), lambda bi, i: (0, 0)),
            pl.BlockSpec((KV_LORA, hs * QK_PAD), lambda bi, i: (0, 0)),
            pl.BlockSpec((ts, LANES), lambda bi, i: (i, 0)),
            pl.BlockSpec((ts, LANES), lambda bi, i: (i, 0)),
        ],
        out_specs=(pl.BlockSpec((1, hs, ts, QK_PAD), lambda bi, i: (bi, 0, i, 0)),
                   pl.BlockSpec((1, hs, ts, QK_PAD), lambda bi, i: (bi, 0, i, 0)),
                   pl.BlockSpec((1, hs, ts, V_PAD), lambda bi, i: (bi, 0, i, 0))),
        compiler_params=_cparams("parallel", "parallel"),
        name="mla_proj",
    )(proj, gq, gkv, wq_ext, wkv, cos2, sin2)


def _attn_kernel(q_ref, k_ref, v_ref, z_ref, o_ref, sa_ref, sb_ref, p_ref, m_ref, acc_ref,
                 *, tq, tk, rb):
    s_len = k_ref.shape[2]
    nq, nk = s_len // tq, s_len // tk
    nt_dims = (((1,), (1,)), ((), ()))
    blocks = [slice(r * rb, (r + 1) * rb) for r in range(tq // rb)]
    lane_tiles = [slice(t * LANES, (t + 1) * LANES) for t in range(tk // LANES)]

    def scores(i, j, s_ref):
        q = q_ref[0, 0, pl.ds(pl.multiple_of(i * tq, tq), tq), :]
        s_ref[...] = lax.dot_general(q, k_ref[0, 0, j * tk:(j + 1) * tk, :], nt_dims,
                                     preferred_element_type=F32)

    def softmax_and_pv(j, s_ref):
        mx_parts = []
        for rows in blocks:
            mx = s_ref[rows, lane_tiles[0]]
            for lt in lane_tiles[1:]:
                mx = jnp.maximum(mx, s_ref[rows, lt])
            mx_parts.append(mx)
        mx_all = jnp.concatenate(mx_parts, axis=0)
        row_max = jnp.broadcast_to(jnp.max(mx_all, axis=-1, keepdims=True), mx_all.shape)
        if j == 0:
            m_new = row_max
        else:
            m_old = m_ref[...]
            m_new = jnp.maximum(m_old, row_max)
            alpha = jnp.exp2(m_old - m_new)
        m_ref[...] = m_new
        for rows in blocks:
            m_b = m_new[rows]
            for lt in lane_tiles:
                p_ref[rows, lt] = jnp.exp2(s_ref[rows, lt] - m_b).astype(BF16)
        pv = jnp.dot(p_ref[...], v_ref[0, 0, j * tk:(j + 1) * tk, :],
                     preferred_element_type=F32)
        if j == 0:
            acc_ref[...] = pv
        else:
            acc_ref[...] = acc_ref[...] * jnp.concatenate([alpha, alpha], axis=1) + pv

    bufs = (sa_ref, sb_ref)
    scores(0, 0, sa_ref)

    def body(i, carry):
        for j in range(nk):
            if j + 1 < nk:
                scores(i, j + 1, bufs[(j + 1) % 2])
            else:
                scores(jnp.minimum(i + 1, nq - 1), 0, bufs[0])
            softmax_and_pv(j, bufs[j % 2])
        rows = pl.ds(pl.multiple_of(i * tq, tq), tq)
        acc = acc_ref[...]
        o_ref[0, rows, :] = ((acc[:, 0:MLA_DV] / acc[:, MLA_DV:])
                             * _silu(z_ref[rows, :])).astype(BF16)
        return carry

    lax.fori_loop(0, nq, body, 0)


def _attention(q, k, v, proj, b, s, tq, tk):
    hs = MLA_HEADS
    zb0 = COL_ZB // LANES
    assert (s // tk) % 2 == 0
    return pl.pallas_call(
        functools.partial(_attn_kernel, tq=tq, tk=tk, rb=2 * SUBLANES),
        out_shape=jax.ShapeDtypeStruct((b, s, MLA_WIDTH), BF16),
        grid=(b, hs),
        in_specs=[
            pl.BlockSpec((1, 1, s, QK_PAD), lambda bi, h: (bi, h, 0, 0)),
            pl.BlockSpec((1, 1, s, QK_PAD), lambda bi, h: (bi, h, 0, 0)),
            pl.BlockSpec((1, 1, s, V_PAD), lambda bi, h: (bi, h, 0, 0)),
            pl.BlockSpec((s, LANES), lambda bi, h: (bi, zb0 + h)),
        ],
        out_specs=pl.BlockSpec((1, s, MLA_DV), lambda bi, h: (bi, 0, h)),
        scratch_shapes=[pltpu.VMEM((tq, tk), F32), pltpu.VMEM((tq, tk), F32),
                        pltpu.VMEM((tq, tk), BF16),
                        pltpu.VMEM((tq, LANES), F32), pltpu.VMEM((tq, V_PAD), F32)],
        compiler_params=_cparams("parallel", "arbitrary"),
        name="attention",
    )(q, k, v, proj)


def _tile_masks():
    r = lax.broadcasted_iota(jnp.int32, (TILE, TILE), 0)
    c = lax.broadcasted_iota(jnp.int32, (TILE, TILE), 1)
    same = (r >> CHUNK_LOG2) == (c >> CHUNK_LOG2)
    return (same & (r >= c)).astype(F32), (same & (r <= c)).astype(F32)


def _col(x, lane):
    li = lax.broadcasted_iota(jnp.int32, x.shape, 1)
    col = jnp.sum(jnp.where(li == lane, x, 0.0), axis=1, keepdims=True)
    return jnp.broadcast_to(col, x.shape)


def _dup_chunks_t(cs):
    parts = []
    for c in range(TILE // CHUNK):
        blk = cs[c * CHUNK:(c + 1) * CHUNK, :]
        parts += [blk, blk]
    return jnp.concatenate(parts, axis=0).T


def _gdn_prep_kernel(q_ref, qp_ref, qn_ref, k_ref, kp_ref, kn_ref, v_ref, vp_ref, vn_ref,
                     cwq_ref, cwk_ref, cwv_ref, gate_ref, gpar_ref,
                     uf_ref, wf_ref, qdf_ref, kdf_ref, atf_ref,
                     ub_ref, wb_ref, qdb_ref, kdb_ref, atb_ref, egl_ref,
                     xq_ref, xk_ref, xv_ref, g_ref, csl_ref, csu_ref, cslt_ref, csut_ref):
    i = pl.program_id(1)
    nt = pl.num_programs(1)
    hg = pl.program_id(2)
    hps = q_ref.shape[1] // LANES

    @pl.when(hg == 0)
    def _():
        t = gate_ref[...]
        a_log = gpar_ref[0:1, :]
        dt_b = gpar_ref[1:2, :]
        li = lax.broadcasted_iota(jnp.int32, t.shape, 1)
        xg = t + dt_b
        sp = jnp.maximum(xg, 0.0) + jnp.log1p(jnp.exp(-jnp.abs(xg)))
        gdec = -jnp.exp(a_log) * sp
        beta = 1.0 / (1.0 + jnp.exp(-t))
        n_gate = 2 * GDN_HEADS
        g = jnp.where(li < n_gate, beta, jnp.where(li < 2 * n_gate, gdec, 0.0))
        g_ref[...] = g
        tri_l, tri_u = _tile_masks()
        hi = g.astype(BF16)
        r1 = g - hi.astype(F32)
        mid = r1.astype(BF16)
        lo = (r1 - mid.astype(F32)).astype(BF16)
        pieces = jnp.concatenate([hi, mid, lo], axis=1)

        def cumsum(tri):
            c3 = jnp.dot(tri.astype(BF16), pieces, preferred_element_type=F32)
            return c3[:, 0:LANES] + c3[:, LANES:2 * LANES] + c3[:, 2 * LANES:]

        csl = cumsum(tri_l)
        csu = cumsum(tri_u)
        csl_ref[...] = csl
        csu_ref[...] = csu
        cslt_ref[...] = _dup_chunks_t(csl)
        csut_ref[...] = _dup_chunks_t(csu)

    def fill_halo(xs_ref, main_ref, prev_ref, next_ref):
        xs_ref[0:SUBLANES, :] = jnp.where(i > 0, prev_ref[...], 0.0)
        xs_ref[SUBLANES:SUBLANES + TILE, :] = main_ref[...]
        xs_ref[SUBLANES + TILE:, :] = jnp.where(i < nt - 1, next_ref[...], 0.0)

    fill_halo(xq_ref, q_ref, qp_ref, qn_ref)
    fill_halo(xk_ref, k_ref, kp_ref, kn_ref)
    fill_halo(xv_ref, v_ref, vp_ref, vn_ref)

    def conv_silu(xs_ref, w_ref, cols):
        pad = (CONV_K - 1) // 2
        xs = xs_ref[:, cols]
        n = xs.shape[0]
        acc = None
        for tap in range(CONV_K):
            sh = xs if tap == pad else pltpu.roll(xs, (pad - tap) % n, axis=0)
            term = sh[SUBLANES:SUBLANES + TILE, :] * w_ref[tap:tap + 1, cols]
            acc = term if acc is None else acc + term
        return _silu(acc)

    def l2n(x):
        return x * lax.rsqrt(jnp.sum(x * x, axis=-1, keepdims=True) + EPS)

    ri = lax.broadcasted_iota(jnp.int32, (CHUNK, LANES), 0)
    li = lax.broadcasted_iota(jnp.int32, (CHUNK, LANES), 1)
    lj = li & (CHUNK - 1)
    left = li < CHUNK
    left_f = left.astype(F32)
    eye_left = (ri == li).astype(F32)
    zeros_r = jnp.zeros((CHUNK, 2 * LANES), BF16)
    nt_dims = (((1,), (1,)), ((), ()))
    nchunk = TILE // CHUNK

    g = g_ref[...]
    csl = csl_ref[...]
    csu = csu_ref[...]
    nh = GDN_HEADS
    lower = (ri >= lj).astype(F32)
    slower = (ri > lj).astype(F32)
    upper = (ri <= lj).astype(F32)
    supper = (ri < lj).astype(F32)

    def load_head(hh):
        cols = slice(hh * LANES, (hh + 1) * LANES)
        q = l2n(conv_silu(xq_ref, cwq_ref, cols)) * GDN_SCALE
        k = l2n(conv_silu(xk_ref, cwk_ref, cols))
        v = conv_silu(xv_ref, cwv_ref, cols)
        return hh, q, k, v

    def setup_head(loaded):
        hh, q, k, v = loaded
        h = hg * hps + hh
        cols = slice(hh * LANES, (hh + 1) * LANES)
        qb = q.astype(BF16)
        kb16 = k.astype(BF16)
        grams = []
        for c in range(nchunk):
            rows = slice(c * CHUNK, (c + 1) * CHUNK)
            k2 = jnp.concatenate([kb16[rows], kb16[rows]], axis=0)
            grams.append((lax.dot_general(kb16[rows], k2, nt_dims, preferred_element_type=F32),
                          lax.dot_general(qb[rows], k2, nt_dims, preferred_element_type=F32)))

        def setup_direction(beta_lane, g_lane, cs, cs_other, cst_ref, tri, stri,
                            u_ref, w_ref, qd_ref, kd_ref, at_ref, egl_row):
            beta = _col(g, beta_lane)
            gc = _col(cs, g_lane)
            rest = _col(cs_other - g, g_lane)
            egc = jnp.exp(gc)
            qd_ref[0, :, cols] = (q * egc).astype(BF16)
            kd_ref[0, :, cols] = (k * jnp.exp(rest)).astype(BF16)
            vb = (v * beta).astype(BF16)
            kbg = (k * (beta * egc)).astype(BF16)
            tot = jnp.exp(gc + rest)
            gc_rows = cst_ref[pl.ds(g_lane, 1), :]
            chains = []
            for c in range(nchunk):
                rows = slice(c * CHUNK, (c + 1) * CHUNK)
                kk2, qk2 = grams[c]
                gc_row = gc_rows[:, c * LANES:(c + 1) * LANES]
                decay = jnp.exp(jnp.minimum(gc[rows] - gc_row, 0.0))
                x = jnp.where(left, eye_left, -(beta[rows] * kk2 * decay * stri))
                half = left_f if c % 2 == 0 else 1.0 - left_f
                at_ref[0, rows, cols] = (qk2 * decay * (tri * half)).astype(BF16)
                egl_ref[0, c, pl.ds(egl_row, 1), :] = tot[c * CHUNK:c * CHUNK + 1, :]
                rhs = jnp.concatenate(
                    [jnp.concatenate([vb[rows], kbg[rows]], axis=1), zeros_r], axis=0)
                chains.append((x, rhs, rows, cols, u_ref, w_ref))
            return chains

        return (setup_direction(h, 2 * nh + h, csl, csu, cslt_ref, lower, slower,
                                uf_ref, wf_ref, qdf_ref, kdf_ref, atf_ref, h)
                + setup_direction(nh + h, 3 * nh + h, csu, csl, csut_ref, upper, supper,
                                  ub_ref, wb_ref, qdb_ref, kdb_ref, atb_ref, nh + h))

    eye_w = eye_left.astype(BF16)

    def invert_and_apply(chains):
        xs = [ch[0].astype(BF16) for ch in chains]
        for _ in range(CHUNK_LOG2):
            xs = [jnp.dot(xb, jnp.concatenate([eye_w, xb], axis=0),
                          preferred_element_type=F32).astype(BF16) for xb in xs]
        for x, (_, rhs, rows, cols, u_ref, w_ref) in zip(xs, chains):
            uw = jnp.dot(x, rhs, preferred_element_type=F32)
            u_ref[0, rows, cols] = uw[:, 0:GDN_DV].astype(BF16)
            w_ref[0, rows, cols] = uw[:, GDN_DV:].astype(BF16)

    group = 2
    pending = None
    for g0 in range(0, hps, group):
        loaded = [load_head(hh) for hh in range(g0, min(g0 + group, hps))]
        if pending is not None:
            invert_and_apply(pending)
        pending = []
        for ld in loaded:
            pending += setup_head(ld)
    invert_and_apply(pending)


def _gdn_prep(proj, conv_w, gpar, b, s, hps):
    nt = s // TILE
    hs = GDN_HEADS
    rows8 = TILE // SUBLANES
    last8 = b * s // SUBLANES - 1
    wid = hps * LANES
    ng = hs // hps

    def main(col0):
        return pl.BlockSpec((TILE, wid), lambda bi, i, h: (bi * nt + i, col0 + h))

    def prev(col0):
        return pl.BlockSpec((SUBLANES, wid),
                            lambda bi, i, h: (jnp.maximum((bi * nt + i) * rows8 - 1, 0), col0 + h))

    def nxt(col0):
        return pl.BlockSpec((SUBLANES, wid),
                            lambda bi, i, h: (jnp.minimum((bi * nt + i + 1) * rows8, last8), col0 + h))

    def cw(col0):
        return pl.BlockSpec((CONV_K, wid), lambda bi, i, h: (0, col0 + h))

    qc, kc, vc = 0, ng, 2 * ng
    seq = jax.ShapeDtypeStruct((b, s, GDN_WIDTH), BF16)
    seq_spec = pl.BlockSpec((1, TILE, wid), lambda bi, i, h: (bi, i, h))
    nchunk = TILE // CHUNK
    egl = jax.ShapeDtypeStruct((b * nt, nchunk, 2 * hs, LANES), F32)
    egl_spec = pl.BlockSpec((1, nchunk, 2 * hs, LANES), lambda bi, i, h: (bi * nt + i, 0, 0, 0))
    return pl.pallas_call(
        _gdn_prep_kernel,
        out_shape=(seq,) * 10 + (egl,),
        grid=(b, nt, ng),
        in_specs=[main(qc), prev(qc), nxt(qc), main(kc), prev(kc), nxt(kc),
                  main(vc), prev(vc), nxt(vc), cw(qc), cw(kc), cw(vc),
                  pl.BlockSpec((TILE, LANES), lambda bi, i, h: (bi * nt + i, COL_GATE // LANES)),
                  pl.BlockSpec((SUBLANES, LANES), lambda bi, i, h: (0, 0))],
        out_specs=(seq_spec,) * 10 + (egl_spec,),
        scratch_shapes=[pltpu.VMEM((TILE + 2 * SUBLANES, wid), F32)] * 3 + [
                        pltpu.VMEM((TILE, LANES), F32), pltpu.VMEM((TILE, LANES), F32),
                        pltpu.VMEM((TILE, LANES), F32), pltpu.VMEM((LANES, 2 * TILE), F32),
                        pltpu.VMEM((LANES, 2 * TILE), F32)],
        compiler_params=_cparams("parallel", "parallel", "arbitrary"),
        name="gdn_prep",
    )(*([proj] * 9), conv_w, conv_w, conv_w, proj, gpar)


def _gdn_scan_kernel(uf_ref, wf_ref, qdf_ref, kdf_ref, atf_ref, eglf_ref,
                     ub_ref, wb_ref, qdb_ref, kdb_ref, atb_ref, eglb_ref,
                     of_ref, ob_ref, st_ref):
    @pl.when(pl.program_id(1) == 0)
    def _():
        st_ref[...] = jnp.zeros_like(st_ref)

    nchunk = TILE // CHUNK
    tn_dims = (((0,), (0,)), ((), ()))

    fwd_refs = (uf_ref, wf_ref, qdf_ref, kdf_ref, atf_ref, eglf_ref, of_ref)
    bwd_refs = (ub_ref, wb_ref, qdb_ref, kdb_ref, atb_ref, eglb_ref, ob_ref)

    def step(c, carry):
        chains = []
        for hd in range(GDN_HEADS):
            chains.append((hd, c, hd, fwd_refs))
            chains.append((GDN_HEADS + hd, nchunk - 1 - c, hd, bwd_refs))
        stage1 = []
        for idx, cc, hd, (u_ref, w_ref, qd_ref, kd_ref, at_ref, egl_ref, o_ref) in chains:
            rows = pl.ds(pl.multiple_of(cc * CHUNK, CHUNK), CHUNK)
            cols = slice(hd * LANES, (hd + 1) * LANES)
            sb = st_ref[idx].astype(BF16)
            wq = jnp.concatenate([w_ref[0, rows, cols], qd_ref[0, rows, cols]], axis=0)
            wqs = jnp.dot(wq, sb, preferred_element_type=F32)
            stage1.append((rows, cols, wqs[0:CHUNK], wqs[CHUNK:]))
        for (idx, cc, hd, refs), (rows, cols, ws, qs) in zip(chains, stage1):
            u_ref, w_ref, qd_ref, kd_ref, at_ref, egl_ref, o_ref = refs
            v_new = (u_ref[0, rows, cols].astype(F32) - ws).astype(BF16)
            v_pair = jnp.concatenate([v_new, v_new], axis=0)
            o_ref[0, rows, cols] = (qs + jnp.dot(at_ref[0, rows, cols], v_pair,
                                                 preferred_element_type=F32)).astype(o_ref.dtype)
            decay = egl_ref[0, pl.ds(cc, 1), idx, :]
            upd = lax.dot_general(kd_ref[0, rows, cols], v_new, tn_dims,
                                  preferred_element_type=F32)
            st_ref[idx] = st_ref[idx] * decay + upd
        return carry

    lax.fori_loop(0, nchunk, step, 0)


def _gdn_scan(prep, b, s):
    uf, wf, qdf, kdf, atf, ub, wb, qdb, kdb, atb, egl = prep
    nt = s // TILE
    hs = GDN_HEADS
    nchunk = TILE // CHUNK
    fwd = pl.BlockSpec((1, TILE, GDN_WIDTH), lambda bi, i: (bi, i, 0))
    bwd = pl.BlockSpec((1, TILE, GDN_WIDTH), lambda bi, i: (bi, nt - 1 - i, 0))
    egl_f = pl.BlockSpec((1, nchunk, 2 * hs, LANES), lambda bi, i: (bi * nt + i, 0, 0, 0))
    egl_b = pl.BlockSpec((1, nchunk, 2 * hs, LANES), lambda bi, i: (bi * nt + nt - 1 - i, 0, 0, 0))
    out = jax.ShapeDtypeStruct((b, s, GDN_WIDTH), BF16)
    return pl.pallas_call(
        _gdn_scan_kernel,
        out_shape=(out, out),
        grid=(b, nt),
        in_specs=[fwd] * 5 + [egl_f] + [bwd] * 5 + [egl_b],
        out_specs=(fwd, bwd),
        scratch_shapes=[pltpu.VMEM((2 * hs, GDN_DK, GDN_DV), F32)],
        compiler_params=_cparams("parallel", "arbitrary"),
        name="gdn_scan",
    )(uf, wf, qdf, kdf, atf, egl, ub, wb, qdb, kdb, atb, egl)


def _outproj_kernel(of_ref, ob_ref, za_ref, mixb_ref, x_ref, gn_ref, gp_ref, w_ref, y_ref):
    o = of_ref[...].astype(F32) + ob_ref[...].astype(F32)
    za = za_ref[...]
    parts = []
    for h in range(GDN_HEADS):
        sl = slice(h * GDN_DV, (h + 1) * GDN_DV)
        parts.append((_rms(o[:, sl], gn_ref[...]) * _silu(za[:, sl])).astype(BF16))
    mix_a = jnp.concatenate(parts, axis=1)
    y = jnp.dot(mix_a, w_ref[0:GDN_WIDTH, :], preferred_element_type=F32)
    y = y + jnp.dot(mixb_ref[...], w_ref[GDN_WIDTH:, :], preferred_element_type=F32)
    y_ref[...] = x_ref[...] + _rms(y, gp_ref[...])


def _outproj(o_f, o_b, proj, mix_b, x2d, gn, gp, w_out, tm):
    t = x2d.shape[0]
    row = lambda i: (i, 0)
    const = lambda i: (0, 0)
    return pl.pallas_call(
        _outproj_kernel,
        out_shape=jax.ShapeDtypeStruct((t, D_MODEL), F32),
        grid=(t // tm,),
        in_specs=[
            pl.BlockSpec((tm, GDN_WIDTH), row),
            pl.BlockSpec((tm, GDN_WIDTH), row),
            pl.BlockSpec((tm, GDN_WIDTH), lambda i: (i, COL_ZA // GDN_WIDTH)),
            pl.BlockSpec((tm, MLA_WIDTH), row),
            pl.BlockSpec((tm, D_MODEL), row),
            pl.BlockSpec((1, GDN_DV), const),
            pl.BlockSpec((1, D_MODEL), const),
            pl.BlockSpec((D_MIX, D_MODEL), const),
        ],
        out_specs=pl.BlockSpec((tm, D_MODEL), row),
        compiler_params=_cparams("parallel"),
        name="outproj",
    )(o_f, o_b, proj, mix_b, x2d, gn, gp, w_out)


def _swap_halves(w):
    half = MLA_ROPE // 2
    return jnp.concatenate([w[..., half:], w[..., :half]], axis=-1)


def _prep_layer(w_in, mla_w_uq, mla_w_ukv, w_out, a_log, dt_bias):
    o = np.cumsum((0, GDN_QKV, GDN_WIDTH, 2 * GDN_HEADS, 2 * GDN_HEADS, Q_LORA, KV_LORA,
                   MLA_ROPE, MLA_WIDTH))
    w_in = w_in.astype(BF16)
    qkv, za, bl, al, cq, ckv, kpe, zb = (w_in[:, o[n]:o[n + 1]] for n in range(8))
    pad = jnp.zeros((D_MODEL, LANES - 4 * GDN_HEADS), BF16)
    w_pad = jnp.concatenate([qkv, za, zb, cq, ckv, kpe, _swap_halves(kpe), bl, al, pad], axis=1)
    wq = mla_w_uq.reshape(Q_LORA, MLA_HEADS, MLA_NOPE + MLA_ROPE)
    wq_ext = jnp.concatenate([wq, _swap_halves(wq[..., MLA_NOPE:])], axis=-1)
    wq_ext = wq_ext.reshape(Q_LORA, MLA_HEADS * QK_PAD).astype(BF16)
    wkv = mla_w_ukv.astype(BF16)
    gpar = jnp.zeros((SUBLANES, LANES), F32)
    n_gate = 2 * GDN_HEADS
    gpar = gpar.at[0, n_gate:2 * n_gate].set(a_log.reshape(-1))
    gpar = gpar.at[1, n_gate:2 * n_gate].set(dt_bias.reshape(-1))
    return w_pad, wq_ext, wkv, w_out.astype(BF16), gpar


def _rope_tables(s):
    pos = jnp.arange(s, dtype=F32)
    inv = ROPE_BASE ** (-jnp.arange(0, MLA_ROPE, 2, dtype=F32) / MLA_ROPE)
    ang = pos[:, None] * inv[None, :]
    cos, sin = jnp.cos(ang), jnp.sin(ang)
    zero = jnp.zeros((s, LANES - MLA_ROPE), F32)
    return (jnp.concatenate([cos, cos, zero], axis=1),
            jnp.concatenate([-sin, sin, zero], axis=1))


def _pick(n, prefs):
    for p in prefs:
        if n % p == 0:
            return p
    return n


def _tiles(b, s):
    t = b * s
    return dict(tm_in=_pick(t, (1024, 512, 256)), tn_in=1024,
                ts_mla=_pick(s, (512, 256)),
                tq=_pick(s, (512, 256, 128)), tk=_pick(s // 2, (1024, 512, 256, 128)),
                tm_out=_pick(t, (512, 256)), gdn_hps=8)


def _layer(x2d, b, s, tl, pre_g, post_g, conv_w, gdn_norm_g, q_norm_g, kv_norm_g,
           w_pad, wq_ext, wkv, w_out, gpar, cos2, sin2):
    proj = _inproj(x2d, pre_g.reshape(1, -1), w_pad, tl["tm_in"], tl["tn_in"])
    prep = _gdn_prep(proj, conv_w, gpar, b, s, tl["gdn_hps"])
    o_f, o_b = _gdn_scan(prep, b, s)
    q, k, v = _mla_proj(proj, q_norm_g.reshape(1, -1), kv_norm_g.reshape(1, -1), wq_ext, wkv,
                        cos2, sin2, b, s, tl["ts_mla"])
    mix_b = _attention(q, k, v, proj, b, s, tl["tq"], tl["tk"])
    return _outproj(o_f.reshape(b * s, -1), o_b.reshape(b * s, -1), proj,
                    mix_b.reshape(b * s, -1), x2d, gdn_norm_g.reshape(1, -1),
                    post_g.reshape(1, -1), w_out, tl["tm_out"])


def _trunk(x, layers, pre_norm_g, post_norm_g, conv_w, gdn_norm_g, mla_q_norm_g, mla_kv_norm_g):
    b, s, d = x.shape
    assert d == D_MODEL and s % TILE == 0
    tl = _tiles(b, s)
    cos2, sin2 = _rope_tables(s)
    x2d = x.reshape(b * s, d)
    for l in range(DEPTH):
        x2d = _layer(x2d, b, s, tl, pre_norm_g[l], post_norm_g[l], conv_w[l], gdn_norm_g[l],
                     mla_q_norm_g[l], mla_kv_norm_g[l], *layers[l], cos2, sin2)
    return x2d.reshape(b, s, d)


def kernel(x_prompt, x_sample, pre_norm_g, post_norm_g, w_in, conv_w, gdn_a_log, gdn_dt_bias,
           gdn_norm_g, mla_q_norm_g, mla_kv_norm_g, mla_w_uq, mla_w_ukv, w_out):
    layers = [_prep_layer(w_in[l], mla_w_uq[l], mla_w_ukv[l], w_out[l], gdn_a_log[l],
                          gdn_dt_bias[l]) for l in range(DEPTH)]
    args = (layers, pre_norm_g, post_norm_g, conv_w, gdn_norm_g, mla_q_norm_g, mla_kv_norm_g)
    return (_trunk(x_prompt, *args), _trunk(x_sample, *args))
```

```python
import functools

import numpy as np
import jax
import jax.numpy as jnp
from jax import lax
from jax.experimental import pallas as pl
from jax.experimental.pallas import tpu as pltpu

F32 = jnp.float32
BF16 = jnp.bfloat16

D_MODEL = 2048
DEPTH = 2
GDN_HEADS = 8
GDN_DK = 128
GDN_DV = 128
GDN_WIDTH = GDN_HEADS * GDN_DV
GDN_QKV = 2 * GDN_HEADS * GDN_DK + GDN_WIDTH
CONV_K = 5
CHUNK = 64
MLA_HEADS = 8
MLA_NOPE = 128
MLA_ROPE = 64
MLA_DV = 128
MLA_WIDTH = MLA_HEADS * MLA_DV
Q_LORA = 512
KV_LORA = 256
ROPE_BASE = 10000.0
D_MIX = GDN_WIDTH + MLA_WIDTH
EPS = 1e-6
MLA_SCALE = (MLA_NOPE + MLA_ROPE) ** -0.5
GDN_SCALE = GDN_DK ** -0.5
LOG2_E = float(np.log2(np.e))

LANES = 128
SUBLANES = 8
VMEM_LIMIT = 56 * 1024 * 1024
ATTN_VMEM_BUDGET = 50 * 1024 * 1024

COL_QKV = 0
COL_ZA = COL_QKV + GDN_QKV
COL_ZB = COL_ZA + GDN_WIDTH
COL_CQ = COL_ZB + MLA_WIDTH
COL_CKV = COL_CQ + Q_LORA
COL_KPE = COL_CKV + KV_LORA
COL_GATE = COL_KPE + 2 * MLA_ROPE
D_PROJ = COL_GATE + LANES
MISC_W = D_PROJ - COL_CQ
QK_PAD = 2 * LANES
V_PAD = 2 * LANES
QK_SCALE_LOG2 = MLA_SCALE * LOG2_E
TILE = 4 * CHUNK
CHUNK_LOG2 = CHUNK.bit_length() - 1
assert 1 << CHUNK_LOG2 == CHUNK and 2 * CHUNK == LANES

assert D_PROJ == 6144 and MISC_W == 1024 and COL_GATE - COL_CQ == 896


def _cparams(*sem):
    return pltpu.CompilerParams(dimension_semantics=sem, vmem_limit_bytes=VMEM_LIMIT)


def _silu(z):
    return z / (1.0 + jnp.exp(-z))


def _rms(x, g):
    return x * lax.rsqrt(jnp.mean(x * x, axis=-1, keepdims=True) + EPS) * g


def _inproj_kernel(x_ref, g_ref, w_ref, o_ref, h_ref):
    @pl.when(pl.program_id(1) == 0)
    def _():
        h_ref[...] = _rms(x_ref[...], g_ref[...]).astype(BF16)

    o_ref[...] = jnp.dot(h_ref[...], w_ref[...], preferred_element_type=F32)


def _inproj(x2d, g, w_pad, tm, tn):
    t = x2d.shape[0]
    return pl.pallas_call(
        _inproj_kernel,
        out_shape=jax.ShapeDtypeStruct((t, D_PROJ), F32),
        grid=(t // tm, D_PROJ // tn),
        in_specs=[
            pl.BlockSpec((tm, D_MODEL), lambda i, j: (i, 0)),
            pl.BlockSpec((1, D_MODEL), lambda i, j: (0, 0)),
            pl.BlockSpec((D_MODEL, tn), lambda i, j: (0, j)),
        ],
        out_specs=pl.BlockSpec((tm, tn), lambda i, j: (i, j)),
        scratch_shapes=[pltpu.VMEM((tm, D_MODEL), BF16)],
        compiler_params=_cparams("parallel", "arbitrary"),
        name="inproj",
    )(x2d, g, w_pad)


def _mla_proj_kernel(p_ref, gq_ref, gkv_ref, wq_ref, wkv_ref, cos_ref, sin_ref,
                     q_ref, k_ref, v_ref):
    t = p_ref[...]
    cqn = _rms(t[:, 0:Q_LORA], gq_ref[...]).astype(BF16)
    ckn = _rms(t[:, Q_LORA:Q_LORA + KV_LORA], gkv_ref[...]).astype(BF16)
    kpe2 = t[:, COL_KPE - COL_CQ:COL_GATE - COL_CQ]
    qe = jnp.dot(cqn, wq_ref[...], preferred_element_type=F32)
    kve = jnp.dot(ckn, wkv_ref[...], preferred_element_type=F32)
    c2 = cos_ref[...]
    s2 = sin_ref[...]
    krope = (kpe2 * c2 + pltpu.roll(kpe2, MLA_ROPE, axis=1) * s2).astype(BF16)
    ones = jnp.ones((t.shape[0], V_PAD - MLA_DV), BF16)
    for h in range(MLA_HEADS):
        qn = qe[:, h * QK_PAD:h * QK_PAD + LANES]
        qp = qe[:, h * QK_PAD + LANES:(h + 1) * QK_PAD]
        qr = qp * c2 + pltpu.roll(qp, MLA_ROPE, axis=1) * s2
        q_ref[0, h, :, 0:LANES] = (qn * QK_SCALE_LOG2).astype(BF16)
        q_ref[0, h, :, LANES:QK_PAD] = (qr * QK_SCALE_LOG2).astype(BF16)
        k_ref[0, h, :, 0:LANES] = kve[:, h * QK_PAD:h * QK_PAD + LANES].astype(BF16)
        k_ref[0, h, :, LANES:QK_PAD] = krope
        v_ref[0, h, :, 0:MLA_DV] = kve[:, h * QK_PAD + LANES:(h + 1) * QK_PAD].astype(BF16)
        v_ref[0, h, :, MLA_DV:] = ones


def _mla_proj(proj, gq, gkv, wq_ext, wkv, cos2, sin2, b, s, ts):
    nt = s // ts
    hs = MLA_HEADS
    return pl.pallas_call(
        _mla_proj_kernel,
        out_shape=(jax.ShapeDtypeStruct((b, hs, s, QK_PAD), BF16),
                   jax.ShapeDtypeStruct((b, hs, s, QK_PAD), BF16),
                   jax.ShapeDtypeStruct((b, hs, s, V_PAD), BF16)),
        grid=(b, nt),
        in_specs=[
            pl.BlockSpec((ts, MISC_W), lambda bi, i: (bi * nt + i, COL_CQ // MISC_W)),
            pl.BlockSpec((1, Q_LORA), lambda bi, i: (0, 0)),
            pl.BlockSpec((1, KV_LORA), lambda bi, i: (0, 0)),
            pl.BlockSpec((Q_LORA, hs * QK_PAD), lambda bi, i: (0, 0)),
            pl.BlockSpec((KV_LORA, hs * QK_PAD), lambda bi, i: (0, 0)),
            pl.BlockSpec((ts, LANES), lambda bi, i: (i, 0)),
            pl.BlockSpec((ts, LANES), lambda bi, i: (i, 0)),
        ],
        out_specs=(pl.BlockSpec((1, hs, ts, QK_PAD), lambda bi, i: (bi, 0, i, 0)),
                   pl.BlockSpec((1, hs, ts, QK_PAD), lambda bi, i: (bi, 0, i, 0)),
                   pl.BlockSpec((1, hs, ts, V_PAD), lambda bi, i: (bi, 0, i, 0))),
        compiler_params=_cparams("parallel", "parallel"),
        name="mla_proj",
    )(proj, gq, gkv, wq_ext, wkv, cos2, sin2)


def _attn_kernel(q_ref, k_ref, v_ref, z_ref, o_ref, sa_ref, sb_ref, p_ref, m_ref, acc_ref,
                 *, tq, tk, rb):
    s_len = k_ref.shape[2]
    nq, nk = s_len // tq, s_len // tk
    nt_dims = (((1,), (1,)), ((), ()))
    blocks = [slice(r * rb, (r + 1) * rb) for r in range(tq // rb)]
    lane_tiles = [slice(t * LANES, (t + 1) * LANES) for t in range(tk // LANES)]

    def scores(i, j, s_ref):
        q = q_ref[0, 0, pl.ds(pl.multiple_of(i * tq, tq), tq), :]
        s_ref[...] = lax.dot_general(q, k_ref[0, 0, j * tk:(j + 1) * tk, :], nt_dims,
                                     preferred_element_type=F32)

    def softmax_and_pv(j, s_ref):
        mx_parts = []
        for rows in blocks:
            mx = s_ref[rows, lane_tiles[0]]
            for lt in lane_tiles[1:]:
                mx = jnp.maximum(mx, s_ref[rows, lt])
            mx_parts.append(mx)
        mx_all = jnp.concatenate(mx_parts, axis=0)
        row_max = jnp.broadcast_to(jnp.max(mx_all, axis=-1, keepdims=True), mx_all.shape)
        if j == 0:
            m_new = row_max
        else:
            m_old = m_ref[...]
            m_new = jnp.maximum(m_old, row_max)
            alpha = jnp.exp2(m_old - m_new)
        m_ref[...] = m_new
        for rows in blocks:
            m_b = m_new[rows]
            for lt in lane_tiles:
                p_ref[rows, lt] = jnp.exp2(s_ref[rows, lt] - m_b).astype(BF16)
        pv = jnp.dot(p_ref[...], v_ref[0, 0, j * tk:(j + 1) * tk, :],
                     preferred_element_type=F32)
        if j == 0:
            acc_ref[...] = pv
        else:
            acc_ref[...] = acc_ref[...] * jnp.concatenate([alpha, alpha], axis=1) + pv

    bufs = (sa_ref, sb_ref)
    scores(0, 0, sa_ref)

    def body(i, carry):
        for j in range(nk):
            if j + 1 < nk:
                scores(i, j + 1, bufs[(j + 1) % 2])
            else:
                scores(jnp.minimum(i + 1, nq - 1), 0, bufs[0])
            softmax_and_pv(j, bufs[j % 2])
        rows = pl.ds(pl.multiple_of(i * tq, tq), tq)
        acc = acc_ref[...]
        o_ref[0, rows, :] = ((acc[:, 0:MLA_DV] / acc[:, MLA_DV:])
                             * _silu(z_ref[rows, :])).astype(BF16)
        return carry

    lax.fori_loop(0, nq, body, 0)


def _attention(q, k, v, proj, b, s, tq, tk):
    hs = MLA_HEADS
    zb0 = COL_ZB // LANES
    assert (s // tk) % 2 == 0
    return pl.pallas_call(
        functools.partial(_attn_kernel, tq=tq, tk=tk, rb=2 * SUBLANES),
        out_shape=jax.ShapeDtypeStruct((b, s, MLA_WIDTH), BF16),
        grid=(b, hs),
        in_specs=[
            pl.BlockSpec((1, 1, s, QK_PAD), lambda bi, h: (bi, h, 0, 0)),
            pl.BlockSpec((1, 1, s, QK_PAD), lambda bi, h: (bi, h, 0, 0)),
            pl.BlockSpec((1, 1, s, V_PAD), lambda bi, h: (bi, h, 0, 0)),
            pl.BlockSpec((s, LANES), lambda bi, h: (bi, zb0 + h)),
        ],
        out_specs=pl.BlockSpec((1, s, MLA_DV), lambda bi, h: (bi, 0, h)),
        scratch_shapes=[pltpu.VMEM((tq, tk), F32), pltpu.VMEM((tq, tk), F32),
                        pltpu.VMEM((tq, tk), BF16),
                        pltpu.VMEM((tq, LANES), F32), pltpu.VMEM((tq, V_PAD), F32)],
        compiler_params=_cparams("parallel", "arbitrary"),
        name="attention",
    )(q, k, v, proj)


def _tile_masks():
    r = lax.broadcasted_iota(jnp.int32, (TILE, TILE), 0)
    c = lax.broadcasted_iota(jnp.int32, (TILE, TILE), 1)
    same = (r >> CHUNK_LOG2) == (c >> CHUNK_LOG2)
    return (same & (r >= c)).astype(F32), (same & (r <= c)).astype(F32)


def _col(x, lane):
    li = lax.broadcasted_iota(jnp.int32, x.shape, 1)
    col = jnp.sum(jnp.where(li == lane, x, 0.0), axis=1, keepdims=True)
    return jnp.broadcast_to(col, x.shape)


def _dup_chunks_t(cs):
    parts = []
    for c in range(TILE // CHUNK):
        blk = cs[c * CHUNK:(c + 1) * CHUNK, :]
        parts += [blk, blk]
    return jnp.concatenate(parts, axis=0).T


def _gdn_prep_kernel(q_ref, qp_ref, qn_ref, k_ref, kp_ref, kn_ref, v_ref, vp_ref, vn_ref,
                     cwq_ref, cwk_ref, cwv_ref, gate_ref, gpar_ref,
                     uf_ref, wf_ref, qdf_ref, kdf_ref, atf_ref,
                     ub_ref, wb_ref, qdb_ref, kdb_ref, atb_ref, egl_ref,
                     xq_ref, xk_ref, xv_ref, g_ref, csl_ref, csu_ref, cslt_ref, csut_ref):
    i = pl.program_id(1)
    nt = pl.num_programs(1)
    hg = pl.program_id(2)
    hps = q_ref.shape[1] // LANES

    @pl.when(hg == 0)
    def _():
        t = gate_ref[...]
        a_log = gpar_ref[0:1, :]
        dt_b = gpar_ref[1:2, :]
        li = lax.broadcasted_iota(jnp.int32, t.shape, 1)
        xg = t + dt_b
        sp = jnp.maximum(xg, 0.0) + jnp.log1p(jnp.exp(-jnp.abs(xg)))
        gdec = -jnp.exp(a_log) * sp
        beta = 1.0 / (1.0 + jnp.exp(-t))
        n_gate = 2 * GDN_HEADS
        g = jnp.where(li < n_gate, beta, jnp.where(li < 2 * n_gate, gdec, 0.0))
        g_ref[...] = g
        tri_l, tri_u = _tile_masks()
        hi = g.astype(BF16)
        r1 = g - hi.astype(F32)
        mid = r1.astype(BF16)
        lo = (r1 - mid.astype(F32)).astype(BF16)
        pieces = jnp.concatenate([hi, mid, lo], axis=1)

        def cumsum(tri):
            c3 = jnp.dot(tri.astype(BF16), pieces, preferred_element_type=F32)
            return c3[:, 0:LANES] + c3[:, LANES:2 * LANES] + c3[:, 2 * LANES:]

        csl = cumsum(tri_l)
        csu = cumsum(tri_u)
        csl_ref[...] = csl
        csu_ref[...] = csu
        cslt_ref[...] = _dup_chunks_t(csl)
        csut_ref[...] = _dup_chunks_t(csu)

    def fill_halo(xs_ref, main_ref, prev_ref, next_ref):
        xs_ref[0:SUBLANES, :] = jnp.where(i > 0, prev_ref[...], 0.0)
        xs_ref[SUBLANES:SUBLANES + TILE, :] = main_ref[...]
        xs_ref[SUBLANES + TILE:, :] = jnp.where(i < nt - 1, next_ref[...], 0.0)

    fill_halo(xq_ref, q_ref, qp_ref, qn_ref)
    fill_halo(xk_ref, k_ref, kp_ref, kn_ref)
    fill_halo(xv_ref, v_ref, vp_ref, vn_ref)

    def conv_silu(xs_ref, w_ref, cols):
        pad = (CONV_K - 1) // 2
        xs = xs_ref[:, cols]
        n = xs.shape[0]
        acc = None
        for tap in range(CONV_K):
            sh = xs if tap == pad else pltpu.roll(xs, (pad - tap) % n, axis=0)
            term = sh[SUBLANES:SUBLANES + TILE, :] * w_ref[tap:tap + 1, cols]
            acc = term if acc is None else acc + term
        return _silu(acc)

    def l2n(x):
        return x * lax.rsqrt(jnp.sum(x * x, axis=-1, keepdims=True) + EPS)

    ri = lax.broadcasted_iota(jnp.int32, (CHUNK, LANES), 0)
    li = lax.broadcasted_iota(jnp.int32, (CHUNK, LANES), 1)
    lj = li & (CHUNK - 1)
    left = li < CHUNK
    left_f = left.astype(F32)
    eye_left = (ri == li).astype(F32)
    zeros_r = jnp.zeros((CHUNK, 2 * LANES), BF16)
    nt_dims = (((1,), (1,)), ((), ()))
    nchunk = TILE // CHUNK

    g = g_ref[...]
    csl = csl_ref[...]
    csu = csu_ref[...]
    nh = GDN_HEADS
    lower = (ri >= lj).astype(F32)
    slower = (ri > lj).astype(F32)
    upper = (ri <= lj).astype(F32)
    supper = (ri < lj).astype(F32)

    def load_head(hh):
        cols = slice(hh * LANES, (hh + 1) * LANES)
        q = l2n(conv_silu(xq_ref, cwq_ref, cols)) * GDN_SCALE
        k = l2n(conv_silu(xk_ref, cwk_ref, cols))
        v = conv_silu(xv_ref, cwv_ref, cols)
        return hh, q, k, v

    def setup_head(loaded):
        hh, q, k, v = loaded
        h = hg * hps + hh
        cols = slice(hh * LANES, (hh + 1) * LANES)
        qb = q.astype(BF16)
        kb16 = k.astype(BF16)
        grams = []
        for c in range(nchunk):
            rows = slice(c * CHUNK, (c + 1) * CHUNK)
            k2 = jnp.concatenate([kb16[rows], kb16[rows]], axis=0)
            grams.append((lax.dot_general(kb16[rows], k2, nt_dims, preferred_element_type=F32),
                          lax.dot_general(qb[rows], k2, nt_dims, preferred_element_type=F32)))

        def setup_direction(beta_lane, g_lane, cs, cs_other, cst_ref, tri, stri,
                            u_ref, w_ref, qd_ref, kd_ref, at_ref, egl_row):
            beta = _col(g, beta_lane)
            gc = _col(cs, g_lane)
            rest = _col(cs_other - g, g_lane)
            egc = jnp.exp(gc)
            qd_ref[0, :, cols] = (q * egc).astype(BF16)
            kd_ref[0, :, cols] = (k * jnp.exp(rest)).astype(BF16)
            vb = (v * beta).astype(BF16)
            kbg = (k * (beta * egc)).astype(BF16)
            tot = jnp.exp(gc + rest)
            gc_rows = cst_ref[pl.ds(g_lane, 1), :]
            chains = []
            for c in range(nchunk):
                rows = slice(c * CHUNK, (c + 1) * CHUNK)
                kk2, qk2 = grams[c]
                gc_row = gc_rows[:, c * LANES:(c + 1) * LANES]
                decay = jnp.exp(jnp.minimum(gc[rows] - gc_row, 0.0))
                x = jnp.where(left, eye_left, -(beta[rows] * kk2 * decay * stri))
                half = left_f if c % 2 == 0 else 1.0 - left_f
                at_ref[0, rows, cols] = (qk2 * decay * (tri * half)).astype(BF16)
                egl_ref[0, c, pl.ds(egl_row, 1), :] = tot[c * CHUNK:c * CHUNK + 1, :]
                rhs = jnp.concatenate(
                    [jnp.concatenate([vb[rows], kbg[rows]], axis=1), zeros_r], axis=0)
                chains.append((x, rhs, rows, cols, u_ref, w_ref))
            return chains

        return (setup_direction(h, 2 * nh + h, csl, csu, cslt_ref, lower, slower,
                                uf_ref, wf_ref, qdf_ref, kdf_ref, atf_ref, h)
                + setup_direction(nh + h, 3 * nh + h, csu, csl, csut_ref, upper, supper,
                                  ub_ref, wb_ref, qdb_ref, kdb_ref, atb_ref, nh + h))

    eye_w = eye_left.astype(BF16)

    def invert_and_apply(chains):
        xs = [ch[0].astype(BF16) for ch in chains]
        for _ in range(CHUNK_LOG2):
            xs = [jnp.dot(xb, jnp.concatenate([eye_w, xb], axis=0),
                          preferred_element_type=F32).astype(BF16) for xb in xs]
        for x, (_, rhs, rows, cols, u_ref, w_ref) in zip(xs, chains):
            uw = jnp.dot(x, rhs, preferred_element_type=F32)
            u_ref[0, rows, cols] = uw[:, 0:GDN_DV].astype(BF16)
            w_ref[0, rows, cols] = uw[:, GDN_DV:].astype(BF16)

    group = 2
    pending = None
    for g0 in range(0, hps, group):
        loaded = [load_head(hh) for hh in range(g0, min(g0 + group, hps))]
        if pending is not None:
            invert_and_apply(pending)
        pending = []
        for ld in loaded:
            pending += setup_head(ld)
    invert_and_apply(pending)


def _gdn_prep(proj, conv_w, gpar, b, s, hps):
    nt = s // TILE
    hs = GDN_HEADS
    rows8 = TILE // SUBLANES
    last8 = b * s // SUBLANES - 1
    wid = hps * LANES
    ng = hs // hps

    def main(col0):
        return pl.BlockSpec((TILE, wid), lambda bi, i, h: (bi * nt + i, col0 + h))

    def prev(col0):
        return pl.BlockSpec((SUBLANES, wid),
                            lambda bi, i, h: (jnp.maximum((bi * nt + i) * rows8 - 1, 0), col0 + h))

    def nxt(col0):
        return pl.BlockSpec((SUBLANES, wid),
                            lambda bi, i, h: (jnp.minimum((bi * nt + i + 1) * rows8, last8), col0 + h))

    def cw(col0):
        return pl.BlockSpec((CONV_K, wid), lambda bi, i, h: (0, col0 + h))

    qc, kc, vc = 0, ng, 2 * ng
    seq = jax.ShapeDtypeStruct((b, s, GDN_WIDTH), BF16)
    seq_spec = pl.BlockSpec((1, TILE, wid), lambda bi, i, h: (bi, i, h))
    nchunk = TILE // CHUNK
    egl = jax.ShapeDtypeStruct((b * nt, nchunk, 2 * hs, LANES), F32)
    egl_spec = pl.BlockSpec((1, nchunk, 2 * hs, LANES), lambda bi, i, h: (bi * nt + i, 0, 0, 0))
    return pl.pallas_call(
        _gdn_prep_kernel,
        out_shape=(seq,) * 10 + (egl,),
        grid=(b, nt, ng),
        in_specs=[main(qc), prev(qc), nxt(qc), main(kc), prev(kc), nxt(kc),
                  main(vc), prev(vc), nxt(vc), cw(qc), cw(kc), cw(vc),
                  pl.BlockSpec((TILE, LANES), lambda bi, i, h: (bi * nt + i, COL_GATE // LANES)),
                  pl.BlockSpec((SUBLANES, LANES), lambda bi, i, h: (0, 0))],
        out_specs=(seq_spec,) * 10 + (egl_spec,),
        scratch_shapes=[pltpu.VMEM((TILE + 2 * SUBLANES, wid), F32)] * 3 + [
                        pltpu.VMEM((TILE, LANES), F32), pltpu.VMEM((TILE, LANES), F32),
                        pltpu.VMEM((TILE, LANES), F32), pltpu.VMEM((LANES, 2 * TILE), F32),
                        pltpu.VMEM((LANES, 2 * TILE), F32)],
        compiler_params=_cparams("parallel", "parallel", "arbitrary"),
        name="gdn_prep",
    )(*([proj] * 9), conv_w, conv_w, conv_w, proj, gpar)


def _gdn_scan_kernel(uf_ref, wf_ref, qdf_ref, kdf_ref, atf_ref, eglf_ref,
                     ub_ref, wb_ref, qdb_ref, kdb_ref, atb_ref, eglb_ref,
                     of_ref, ob_ref, st_ref):
    @pl.when(pl.program_id(1) == 0)
    def _():
        st_ref[...] = jnp.zeros_like(st_ref)

    nchunk = TILE // CHUNK
    tn_dims = (((0,), (0,)), ((), ()))

    fwd_refs = (uf_ref, wf_ref, qdf_ref, kdf_ref, atf_ref, eglf_ref, of_ref)
    bwd_refs = (ub_ref, wb_ref, qdb_ref, kdb_ref, atb_ref, eglb_ref, ob_ref)

    def step(c, carry):
        chains = []
        for hd in range(GDN_HEADS):
            chains.append((hd, c, hd, fwd_refs))
            chains.append((GDN_HEADS + hd, nchunk - 1 - c, hd, bwd_refs))
        stage1 = []
        for idx, cc, hd, (u_ref, w_ref, qd_ref, kd_ref, at_ref, egl_ref, o_ref) in chains:
            rows = pl.ds(pl.multiple_of(cc * CHUNK, CHUNK), CHUNK)
            cols = slice(hd * LANES, (hd + 1) * LANES)
            sb = st_ref[idx].astype(BF16)
            wq = jnp.concatenate([w_ref[0, rows, cols], qd_ref[0, rows, cols]], axis=0)
            wqs = jnp.dot(wq, sb, preferred_element_type=F32)
            stage1.append((rows, cols, wqs[0:CHUNK], wqs[CHUNK:]))
        for (idx, cc, hd, refs), (rows, cols, ws, qs) in zip(chains, stage1):
            u_ref, w_ref, qd_ref, kd_ref, at_ref, egl_ref, o_ref = refs
            v_new = (u_ref[0, rows, cols].astype(F32) - ws).astype(BF16)
            v_pair = jnp.concatenate([v_new, v_new], axis=0)
            o_ref[0, rows, cols] = (qs + jnp.dot(at_ref[0, rows, cols], v_pair,
                                                 preferred_element_type=F32)).astype(o_ref.dtype)
            decay = egl_ref[0, pl.ds(cc, 1), idx, :]
            upd = lax.dot_general(kd_ref[0, rows, cols], v_new, tn_dims,
                                  preferred_element_type=F32)
            st_ref[idx] = st_ref[idx] * decay + upd
        return carry

    lax.fori_loop(0, nchunk, step, 0)


def _gdn_scan(prep, b, s):
    uf, wf, qdf, kdf, atf, ub, wb, qdb, kdb, atb, egl = prep
    nt = s // TILE
    hs = GDN_HEADS
    nchunk = TILE // CHUNK
    fwd = pl.BlockSpec((1, TILE, GDN_WIDTH), lambda bi, i: (bi, i, 0))
    bwd = pl.BlockSpec((1, TILE, GDN_WIDTH), lambda bi, i: (bi, nt - 1 - i, 0))
    egl_f = pl.BlockSpec((1, nchunk, 2 * hs, LANES), lambda bi, i: (bi * nt + i, 0, 0, 0))
    egl_b = pl.BlockSpec((1, nchunk, 2 * hs, LANES), lambda bi, i: (bi * nt + nt - 1 - i, 0, 0, 0))
    out = jax.ShapeDtypeStruct((b, s, GDN_WIDTH), BF16)
    return pl.pallas_call(
        _gdn_scan_kernel,
        out_shape=(out, out),
        grid=(b, nt),
        in_specs=[fwd] * 5 + [egl_f] + [bwd] * 5 + [egl_b],
        out_specs=(fwd, bwd),
        scratch_shapes=[pltpu.VMEM((2 * hs, GDN_DK, GDN_DV), F32)],
        compiler_params=_cparams("parallel", "arbitrary"),
        name="gdn_scan",
    )(uf, wf, qdf, kdf, atf, egl, ub, wb, qdb, kdb, atb, egl)


def _outproj_kernel(of_ref, ob_ref, za_ref, mixb_ref, x_ref, gn_ref, gp_ref, w_ref, y_ref):
    o = of_ref[...].astype(F32) + ob_ref[...].astype(F32)
    za = za_ref[...]
    parts = []
    for h in range(GDN_HEADS):
        sl = slice(h * GDN_DV, (h + 1) * GDN_DV)
        parts.append((_rms(o[:, sl], gn_ref[...]) * _silu(za[:, sl])).astype(BF16))
    mix_a = jnp.concatenate(parts, axis=1)
    y = jnp.dot(mix_a, w_ref[0:GDN_WIDTH, :], preferred_element_type=F32)
    y = y + jnp.dot(mixb_ref[...], w_ref[GDN_WIDTH:, :], preferred_element_type=F32)
    y_ref[...] = x_ref[...] + _rms(y, gp_ref[...])


def _outproj(o_f, o_b, proj, mix_b, x2d, gn, gp, w_out, tm):
    t = x2d.shape[0]
    row = lambda i: (i, 0)
    const = lambda i: (0, 0)
    return pl.pallas_call(
        _outproj_kernel,
        out_shape=jax.ShapeDtypeStruct((t, D_MODEL), F32),
        grid=(t // tm,),
        in_specs=[
            pl.BlockSpec((tm, GDN_WIDTH), row),
            pl.BlockSpec((tm, GDN_WIDTH), row),
            pl.BlockSpec((tm, GDN_WIDTH), lambda i: (i, COL_ZA // GDN_WIDTH)),
            pl.BlockSpec((tm, MLA_WIDTH), row),
            pl.BlockSpec((tm, D_MODEL), row),
            pl.BlockSpec((1, GDN_DV), const),
            pl.BlockSpec((1, D_MODEL), const),
            pl.BlockSpec((D_MIX, D_MODEL), const),
        ],
        out_specs=pl.BlockSpec((tm, D_MODEL), row),
        compiler_params=_cparams("parallel"),
        name="outproj",
    )(o_f, o_b, proj, mix_b, x2d, gn, gp, w_out)


def _swap_halves(w):
    half = MLA_ROPE // 2
    return jnp.concatenate([w[..., half:], w[..., :half]], axis=-1)


def _prep_layer(w_in, mla_w_uq, mla_w_ukv, w_out, a_log, dt_bias):
    o = np.cumsum((0, GDN_QKV, GDN_WIDTH, 2 * GDN_HEADS, 2 * GDN_HEADS, Q_LORA, KV_LORA,
                   MLA_ROPE, MLA_WIDTH))
    w_in = w_in.astype(BF16)
    qkv, za, bl, al, cq, ckv, kpe, zb = (w_in[:, o[n]:o[n + 1]] for n in range(8))
    pad = jnp.zeros((D_MODEL, LANES - 4 * GDN_HEADS), BF16)
    w_pad = jnp.concatenate([qkv, za, zb, cq, ckv, kpe, _swap_halves(kpe), bl, al, pad], axis=1)
    wq = mla_w_uq.reshape(Q_LORA, MLA_HEADS, MLA_NOPE + MLA_ROPE)
    wq_ext = jnp.concatenate([wq, _swap_halves(wq[..., MLA_NOPE:])], axis=-1)
    wq_ext = wq_ext.reshape(Q_LORA, MLA_HEADS * QK_PAD).astype(BF16)
    wkv = mla_w_ukv.astype(BF16)
    gpar = jnp.zeros((SUBLANES, LANES), F32)
    n_gate = 2 * GDN_HEADS
    gpar = gpar.at[0, n_gate:2 * n_gate].set(a_log.reshape(-1))
    gpar = gpar.at[1, n_gate:2 * n_gate].set(dt_bias.reshape(-1))
    return w_pad, wq_ext, wkv, w_out.astype(BF16), gpar


def _rope_tables(s):
    pos = jnp.arange(s, dtype=F32)
    inv = ROPE_BASE ** (-jnp.arange(0, MLA_ROPE, 2, dtype=F32) / MLA_ROPE)
    ang = pos[:, None] * inv[None, :]
    cos, sin = jnp.cos(ang), jnp.sin(ang)
    zero = jnp.zeros((s, LANES - MLA_ROPE), F32)
    return (jnp.concatenate([cos, cos, zero], axis=1),
            jnp.concatenate([-sin, sin, zero], axis=1))


def _pick(n, prefs):
    for p in prefs:
        if n % p == 0:
            return p
    return n


def _attn_tiles(s):
    tq = _pick(s, (1024, 512, 256, 128))
    resident = 2 * s * ((2 * QK_PAD + V_PAD) * 2 + LANES * 4 + MLA_DV * 2)
    for tk in (2048, 1024, 512, 256, 128):
        scratch = tq * tk * (4 + 4 + 2)
        if (s // 2) % tk == 0 and resident + scratch <= ATTN_VMEM_BUDGET:
            return tq, tk
    raise ValueError(f"no attention tiling for sequence length {s}")


def _tiles(b, s):
    t = b * s
    tq, tk = _attn_tiles(s)
    return dict(tm_in=_pick(t, (1024, 512, 256)), tn_in=1024,
                ts_mla=_pick(s, (512, 256)), tq=tq, tk=tk,
                tm_out=_pick(t, (512, 256)), gdn_hps=8)


def _layer(x2d, b, s, tl, pre_g, post_g, conv_w, gdn_norm_g, q_norm_g, kv_norm_g,
           w_pad, wq_ext, wkv, w_out, gpar, cos2, sin2):
    proj = _inproj(x2d, pre_g.reshape(1, -1), w_pad, tl["tm_in"], tl["tn_in"])
    prep = _gdn_prep(proj, conv_w, gpar, b, s, tl["gdn_hps"])
    o_f, o_b = _gdn_scan(prep, b, s)
    q, k, v = _mla_proj(proj, q_norm_g.reshape(1, -1), kv_norm_g.reshape(1, -1), wq_ext, wkv,
                        cos2, sin2, b, s, tl["ts_mla"])
    mix_b = _attention(q, k, v, proj, b, s, tl["tq"], tl["tk"])
    return _outproj(o_f.reshape(b * s, -1), o_b.reshape(b * s, -1), proj,
                    mix_b.reshape(b * s, -1), x2d, gdn_norm_g.reshape(1, -1),
                    post_g.reshape(1, -1), w_out, tl["tm_out"])


def _trunk(x, layers, pre_norm_g, post_norm_g, conv_w, gdn_norm_g, mla_q_norm_g, mla_kv_norm_g):
    b, s, d = x.shape
    assert d == D_MODEL and s % TILE == 0
    tl = _tiles(b, s)
    cos2, sin2 = _rope_tables(s)
    x2d = x.reshape(b * s, d)
    for l in range(DEPTH):
        x2d = _layer(x2d, b, s, tl, pre_norm_g[l], post_norm_g[l], conv_w[l], gdn_norm_g[l],
                     mla_q_norm_g[l], mla_kv_norm_g[l], *layers[l], cos2, sin2)
    return x2d.reshape(b, s, d)


def kernel(x_prompt, x_sample, pre_norm_g, post_norm_g, w_in, conv_w, gdn_a_log, gdn_dt_bias,
           gdn_norm_g, mla_q_norm_g, mla_kv_norm_g, mla_w_uq, mla_w_ukv, w_out):
    layers = [_prep_layer(w_in[l], mla_w_uq[l], mla_w_ukv[l], w_out[l], gdn_a_log[l],
                          gdn_dt_bias[l]) for l in range(DEPTH)]
    args = (layers, pre_norm_g, post_norm_g, conv_w, gdn_norm_g, mla_q_norm_g, mla_kv_norm_g)
    return (_trunk(x_prompt, *args), _trunk(x_sample, *args))
```

```python
import functools

import numpy as np
import jax
import jax.numpy as jnp
from jax import lax
from jax.experimental import pallas as pl
from jax.experimental.pallas import tpu as pltpu

F32 = jnp.float32
BF16 = jnp.bfloat16

D_MODEL = 2048
DEPTH = 2
GDN_HEADS = 8
GDN_DK = 128
GDN_DV = 128
GDN_WIDTH = GDN_HEADS * GDN_DV
GDN_QKV = 2 * GDN_HEADS * GDN_DK + GDN_WIDTH
CONV_K = 5
CHUNK = 64
MLA_HEADS = 8
MLA_NOPE = 128
MLA_ROPE = 64
MLA_DV = 128
MLA_WIDTH = MLA_HEADS * MLA_DV
Q_LORA = 512
KV_LORA = 256
ROPE_BASE = 10000.0
D_MIX = GDN_WIDTH + MLA_WIDTH
EPS = 1e-6
MLA_SCALE = (MLA_NOPE + MLA_ROPE) ** -0.5
GDN_SCALE = GDN_DK ** -0.5
LOG2_E = float(np.log2(np.e))

LANES = 128
SUBLANES = 8
VMEM_LIMIT = 56 * 1024 * 1024
ATTN_VMEM_BUDGET = 50 * 1024 * 1024

COL_QKV = 0
COL_ZA = COL_QKV + GDN_QKV
COL_ZB = COL_ZA + GDN_WIDTH
COL_CQ = COL_ZB + MLA_WIDTH
COL_CKV = COL_CQ + Q_LORA
COL_KPE = COL_CKV + KV_LORA
COL_GATE = COL_KPE + 2 * MLA_ROPE
D_PROJ = COL_GATE + LANES
MISC_W = D_PROJ - COL_CQ
QK_PAD = 2 * LANES
V_PAD = 2 * LANES
QK_SCALE_LOG2 = MLA_SCALE * LOG2_E
TILE = 4 * CHUNK
CHUNK_LOG2 = CHUNK.bit_length() - 1
assert 1 << CHUNK_LOG2 == CHUNK and 2 * CHUNK == LANES

assert D_PROJ == 6144 and MISC_W == 1024 and COL_GATE - COL_CQ == 896


def _cparams(*sem):
    return pltpu.CompilerParams(dimension_semantics=sem, vmem_limit_bytes=VMEM_LIMIT)


def _silu(z):
    return z / (1.0 + jnp.exp(-z))


def _rms(x, g):
    return x * lax.rsqrt(jnp.mean(x * x, axis=-1, keepdims=True) + EPS) * g


def _inproj_kernel(x_ref, g_ref, w_ref, o_ref, h_ref):
    @pl.when(pl.program_id(1) == 0)
    def _():
        h_ref[...] = _rms(x_ref[...], g_ref[...]).astype(BF16)

    o_ref[...] = jnp.dot(h_ref[...], w_ref[...], preferred_element_type=F32)


def _inproj(x2d, g, w_pad, tm, tn):
    t = x2d.shape[0]
    return pl.pallas_call(
        _inproj_kernel,
        out_shape=jax.ShapeDtypeStruct((t, D_PROJ), F32),
        grid=(t // tm, D_PROJ // tn),
        in_specs=[
            pl.BlockSpec((tm, D_MODEL), lambda i, j: (i, 0)),
            pl.BlockSpec((1, D_MODEL), lambda i, j: (0, 0)),
            pl.BlockSpec((D_MODEL, tn), lambda i, j: (0, j)),
        ],
        out_specs=pl.BlockSpec((tm, tn), lambda i, j: (i, j)),
        scratch_shapes=[pltpu.VMEM((tm, D_MODEL), BF16)],
        compiler_params=_cparams("parallel", "arbitrary"),
        name="inproj",
    )(x2d, g, w_pad)


def _mla_proj_kernel(p_ref, gq_ref, gkv_ref, wq_ref, wkv_ref, cos_ref, sin_ref,
                     q_ref, k_ref, v_ref):
    t = p_ref[...]
    cqn = _rms(t[:, 0:Q_LORA], gq_ref[...]).astype(BF16)
    ckn = _rms(t[:, Q_LORA:Q_LORA + KV_LORA], gkv_ref[...]).astype(BF16)
    kpe2 = t[:, COL_KPE - COL_CQ:COL_GATE - COL_CQ]
    qe = jnp.dot(cqn, wq_ref[...], preferred_element_type=F32)
    kve = jnp.dot(ckn, wkv_ref[...], preferred_element_type=F32)
    c2 = cos_ref[...]
    s2 = sin_ref[...]
    krope = (kpe2 * c2 + pltpu.roll(kpe2, MLA_ROPE, axis=1) * s2).astype(BF16)
    ones = jnp.ones((t.shape[0], V_PAD - MLA_DV), BF16)
    for h in range(MLA_HEADS):
        qn = qe[:, h * QK_PAD:h * QK_PAD + LANES]
        qp = qe[:, h * QK_PAD + LANES:(h + 1) * QK_PAD]
        qr = qp * c2 + pltpu.roll(qp, MLA_ROPE, axis=1) * s2
        q_ref[0, h, :, 0:LANES] = (qn * QK_SCALE_LOG2).astype(BF16)
        q_ref[0, h, :, LANES:QK_PAD] = (qr * QK_SCALE_LOG2).astype(BF16)
        k_ref[0, h, :, 0:LANES] = kve[:, h * QK_PAD:h * QK_PAD + LANES].astype(BF16)
        k_ref[0, h, :, LANES:QK_PAD] = krope
        v_ref[0, h, :, 0:MLA_DV] = kve[:, h * QK_PAD + LANES:(h + 1) * QK_PAD].astype(BF16)
        v_ref[0, h, :, MLA_DV:] = ones


def _mla_proj(proj, gq, gkv, wq_ext, wkv, cos2, sin2, b, s, ts):
    nt = s // ts
    hs = MLA_HEADS
    return pl.pallas_call(
        _mla_proj_kernel,
        out_shape=(jax.ShapeDtypeStruct((b, hs, s, QK_PAD), BF16),
                   jax.ShapeDtypeStruct((b, hs, s, QK_PAD), BF16),
                   jax.ShapeDtypeStruct((b, hs, s, V_PAD), BF16)),
        grid=(b, nt),
        in_specs=[
            pl.BlockSpec((ts, MISC_W), lambda bi, i: (bi * nt + i, COL_CQ // MISC_W)),
            pl.BlockSpec((1, Q_LORA), lambda bi, i: (0, 0)),
            pl.BlockSpec((1, KV_LORA), lambda bi, i: (0, 0)),
            pl.BlockSpec((Q_LORA, hs * QK_PAD), lambda bi, i: (0, 0)),
            pl.BlockSpec((KV_LORA, hs * QK_PAD), lambda bi, i: (0, 0)),
            pl.BlockSpec((ts, LANES), lambda bi, i: (i, 0)),
            pl.BlockSpec((ts, LANES), lambda bi, i: (i, 0)),
        ],
        out_specs=(pl.BlockSpec((1, hs, ts, QK_PAD), lambda bi, i: (bi, 0, i, 0)),
                   pl.BlockSpec((1, hs, ts, QK_PAD), lambda bi, i: (bi, 0, i, 0)),
                   pl.BlockSpec((1, hs, ts, V_PAD), lambda bi, i: (bi, 0, i, 0))),
        compiler_params=_cparams("parallel", "parallel"),
        name="mla_proj",
    )(proj, gq, gkv, wq_ext, wkv, cos2, sin2)


def _attn_kernel(q_ref, k_ref, v_ref, z_ref, o_ref, sa_ref, sb_ref, p_ref, m_ref, acc_ref,
                 *, tq, tk, rb):
    s_len = k_ref.shape[2]
    nq, nk = s_len // tq, s_len // tk
    nt_dims = (((1,), (1,)), ((), ()))
    blocks = [slice(r * rb, (r + 1) * rb) for r in range(tq // rb)]
    lane_tiles = [slice(t * LANES, (t + 1) * LANES) for t in range(tk // LANES)]

    def scores(i, j, s_ref):
        q = q_ref[0, 0, pl.ds(pl.multiple_of(i * tq, tq), tq), :]
        s_ref[...] = lax.dot_general(q, k_ref[0, 0, j * tk:(j + 1) * tk, :], nt_dims,
                                     preferred_element_type=F32)

    def softmax_and_pv(j, s_ref):
        mx_parts = []
        for rows in blocks:
            mx = s_ref[rows, lane_tiles[0]]
            for lt in lane_tiles[1:]:
                mx = jnp.maximum(mx, s_ref[rows, lt])
            mx_parts.append(mx)
        mx_all = jnp.concatenate(mx_parts, axis=0)
        row_max = jnp.broadcast_to(jnp.max(mx_all, axis=-1, keepdims=True), mx_all.shape)
        if j == 0:
            m_new = row_max
        else:
            m_old = m_ref[...]
            m_new = jnp.maximum(m_old, row_max)
            alpha = jnp.exp2(m_old - m_new)
        m_ref[...] = m_new
        for rows in blocks:
            m_b = m_new[rows]
            for lt in lane_tiles:
                p_ref[rows, lt] = jnp.exp2(s_ref[rows, lt] - m_b).astype(BF16)
        pv = jnp.dot(p_ref[...], v_ref[0, 0, j * tk:(j + 1) * tk, :],
                     preferred_element_type=F32)
        if j == 0:
            acc_ref[...] = pv
        else:
            acc_ref[...] = acc_ref[...] * jnp.concatenate([alpha, alpha], axis=1) + pv

    bufs = (sa_ref, sb_ref)
    scores(0, 0, sa_ref)

    def body(i, carry):
        for j in range(nk):
            if j + 1 < nk:
                scores(i, j + 1, bufs[(j + 1) % 2])
            else:
                scores(jnp.minimum(i + 1, nq - 1), 0, bufs[0])
            softmax_and_pv(j, bufs[j % 2])
        rows = pl.ds(pl.multiple_of(i * tq, tq), tq)
        acc = acc_ref[...]
        o_ref[0, rows, :] = ((acc[:, 0:MLA_DV] / acc[:, MLA_DV:])
                             * _silu(z_ref[rows, :])).astype(BF16)
        return carry

    lax.fori_loop(0, nq, body, 0)


def _attention(q, k, v, proj, b, s, tq, tk):
    hs = MLA_HEADS
    zb0 = COL_ZB // LANES
    assert (s // tk) % 2 == 0
    return pl.pallas_call(
        functools.partial(_attn_kernel, tq=tq, tk=tk, rb=2 * SUBLANES),
        out_shape=jax.ShapeDtypeStruct((b, s, MLA_WIDTH), BF16),
        grid=(b, hs),
        in_specs=[
            pl.BlockSpec((1, 1, s, QK_PAD), lambda bi, h: (bi, h, 0, 0)),
            pl.BlockSpec((1, 1, s, QK_PAD), lambda bi, h: (bi, h, 0, 0)),
            pl.BlockSpec((1, 1, s, V_PAD), lambda bi, h: (bi, h, 0, 0)),
            pl.BlockSpec((s, LANES), lambda bi, h: (bi, zb0 + h)),
        ],
        out_specs=pl.BlockSpec((1, s, MLA_DV), lambda bi, h: (bi, 0, h)),
        scratch_shapes=[pltpu.VMEM((tq, tk), F32), pltpu.VMEM((tq, tk), F32),
                        pltpu.VMEM((tq, tk), BF16),
                        pltpu.VMEM((tq, LANES), F32), pltpu.VMEM((tq, V_PAD), F32)],
        compiler_params=_cparams("parallel", "arbitrary"),
        name="attention",
    )(q, k, v, proj)


def _tile_masks():
    r = lax.broadcasted_iota(jnp.int32, (TILE, TILE), 0)
    c = lax.broadcasted_iota(jnp.int32, (TILE, TILE), 1)
    same = (r >> CHUNK_LOG2) == (c >> CHUNK_LOG2)
    return (same & (r >= c)).astype(F32), (same & (r <= c)).astype(F32)


def _col(x, lane):
    li = lax.broadcasted_iota(jnp.int32, x.shape, 1)
    col = jnp.sum(jnp.where(li == lane, x, 0.0), axis=1, keepdims=True)
    return jnp.broadcast_to(col, x.shape)


def _dup_chunks_t(cs):
    parts = []
    for c in range(TILE // CHUNK):
        blk = cs[c * CHUNK:(c + 1) * CHUNK, :]
        parts += [blk, blk]
    return jnp.concatenate(parts, axis=0).T


def _gdn_prep_kernel(q_ref, qp_ref, qn_ref, k_ref, kp_ref, kn_ref, v_ref, vp_ref, vn_ref,
                     cwq_ref, cwk_ref, cwv_ref, gate_ref, gpar_ref,
                     uf_ref, wf_ref, qdf_ref, kdf_ref, atf_ref,
                     ub_ref, wb_ref, qdb_ref, kdb_ref, atb_ref, egl_ref,
                     xq_ref, xk_ref, xv_ref, g_ref, csl_ref, csu_ref, cslt_ref, csut_ref):
    i = pl.program_id(1)
    nt = pl.num_programs(1)
    hg = pl.program_id(2)
    hps = q_ref.shape[1] // LANES

    @pl.when(hg == 0)
    def _():
        t = gate_ref[...]
        a_log = gpar_ref[0:1, :]
        dt_b = gpar_ref[1:2, :]
        li = lax.broadcasted_iota(jnp.int32, t.shape, 1)
        xg = t + dt_b
        sp = jnp.maximum(xg, 0.0) + jnp.log1p(jnp.exp(-jnp.abs(xg)))
        gdec = -jnp.exp(a_log) * sp
        beta = 1.0 / (1.0 + jnp.exp(-t))
        n_gate = 2 * GDN_HEADS
        g = jnp.where(li < n_gate, beta, jnp.where(li < 2 * n_gate, gdec, 0.0))
        g_ref[...] = g
        tri_l, tri_u = _tile_masks()
        hi = g.astype(BF16)
        r1 = g - hi.astype(F32)
        mid = r1.astype(BF16)
        lo = (r1 - mid.astype(F32)).astype(BF16)
        pieces = jnp.concatenate([hi, mid, lo], axis=1)

        def cumsum(tri):
            c3 = jnp.dot(tri.astype(BF16), pieces, preferred_element_type=F32)
            return c3[:, 0:LANES] + c3[:, LANES:2 * LANES] + c3[:, 2 * LANES:]

        csl = cumsum(tri_l)
        csu = cumsum(tri_u)
        csl_ref[...] = csl
        csu_ref[...] = csu
        cslt_ref[...] = _dup_chunks_t(csl)
        csut_ref[...] = _dup_chunks_t(csu)

    def fill_halo(xs_ref, main_ref, prev_ref, next_ref):
        xs_ref[0:SUBLANES, :] = jnp.where(i > 0, prev_ref[...], 0.0)
        xs_ref[SUBLANES:SUBLANES + TILE, :] = main_ref[...]
        xs_ref[SUBLANES + TILE:, :] = jnp.where(i < nt - 1, next_ref[...], 0.0)

    fill_halo(xq_ref, q_ref, qp_ref, qn_ref)
    fill_halo(xk_ref, k_ref, kp_ref, kn_ref)
    fill_halo(xv_ref, v_ref, vp_ref, vn_ref)

    def conv_silu(xs_ref, w_ref, cols):
        pad = (CONV_K - 1) // 2
        xs = xs_ref[:, cols]
        n = xs.shape[0]
        acc = None
        for tap in range(CONV_K):
            sh = xs if tap == pad else pltpu.roll(xs, (pad - tap) % n, axis=0)
            term = sh[SUBLANES:SUBLANES + TILE, :] * w_ref[tap:tap + 1, cols]
            acc = term if acc is None else acc + term
        return _silu(acc)

    def l2n(x):
        return x * lax.rsqrt(jnp.sum(x * x, axis=-1, keepdims=True) + EPS)

    ri = lax.broadcasted_iota(jnp.int32, (CHUNK, LANES), 0)
    li = lax.broadcasted_iota(jnp.int32, (CHUNK, LANES), 1)
    lj = li & (CHUNK - 1)
    left = li < CHUNK
    left_f = left.astype(F32)
    eye_left = (ri == li).astype(F32)
    zeros_r = jnp.zeros((CHUNK, 2 * LANES), BF16)
    nt_dims = (((1,), (1,)), ((), ()))
    nchunk = TILE // CHUNK

    g = g_ref[...]
    csl = csl_ref[...]
    csu = csu_ref[...]
    nh = GDN_HEADS
    lower = (ri >= lj).astype(F32)
    slower = (ri > lj).astype(F32)
    upper = (ri <= lj).astype(F32)
    supper = (ri < lj).astype(F32)

    def load_head(hh):
        cols = slice(hh * LANES, (hh + 1) * LANES)
        q = l2n(conv_silu(xq_ref, cwq_ref, cols)) * GDN_SCALE
        k = l2n(conv_silu(xk_ref, cwk_ref, cols))
        v = conv_silu(xv_ref, cwv_ref, cols)
        return hh, q, k, v

    def setup_head(loaded):
        hh, q, k, v = loaded
        h = hg * hps + hh
        cols = slice(hh * LANES, (hh + 1) * LANES)
        qb = q.astype(BF16)
        kb16 = k.astype(BF16)
        grams = []
        for c in range(nchunk):
            rows = slice(c * CHUNK, (c + 1) * CHUNK)
            k2 = jnp.concatenate([kb16[rows], kb16[rows]], axis=0)
            grams.append((lax.dot_general(kb16[rows], k2, nt_dims, preferred_element_type=F32),
                          lax.dot_general(qb[rows], k2, nt_dims, preferred_element_type=F32)))

        def setup_direction(beta_lane, g_lane, cs, cs_other, cst_ref, tri, stri,
                            u_ref, w_ref, qd_ref, kd_ref, at_ref, egl_row):
            beta = _col(g, beta_lane)
            gc = _col(cs, g_lane)
            rest = _col(cs_other - g, g_lane)
            egc = jnp.exp(gc)
            qd_ref[0, :, cols] = (q * egc).astype(BF16)
            kd_ref[0, :, cols] = (k * jnp.exp(rest)).astype(BF16)
            vb = (v * beta).astype(BF16)
            kbg = (k * (beta * egc)).astype(BF16)
            tot = jnp.exp(gc + rest)
            gc_rows = cst_ref[pl.ds(g_lane, 1), :]
            chains = []
            for c in range(nchunk):
                rows = slice(c * CHUNK, (c + 1) * CHUNK)
                kk2, qk2 = grams[c]
                gc_row = gc_rows[:, c * LANES:(c + 1) * LANES]
                decay = jnp.exp(jnp.minimum(gc[rows] - gc_row, 0.0))
                x = jnp.where(left, eye_left, -(beta[rows] * kk2 * decay * stri))
                half = left_f if c % 2 == 0 else 1.0 - left_f
                at_ref[0, rows, cols] = (qk2 * decay * (tri * half)).astype(BF16)
                egl_ref[0, c, pl.ds(egl_row, 1), :] = tot[c * CHUNK:c * CHUNK + 1, :]
                rhs = jnp.concatenate(
                    [jnp.concatenate([vb[rows], kbg[rows]], axis=1), zeros_r], axis=0)
                chains.append((x, rhs, rows, cols, u_ref, w_ref))
            return chains

        return (setup_direction(h, 2 * nh + h, csl, csu, cslt_ref, lower, slower,
                                uf_ref, wf_ref, qdf_ref, kdf_ref, atf_ref, h)
                + setup_direction(nh + h, 3 * nh + h, csu, csl, csut_ref, upper, supper,
                                  ub_ref, wb_ref, qdb_ref, kdb_ref, atb_ref, nh + h))

    eye_w = eye_left.astype(BF16)

    def invert_and_apply(chains):
        xs = [ch[0].astype(BF16) for ch in chains]
        for _ in range(CHUNK_LOG2):
            xs = [jnp.dot(xb, jnp.concatenate([eye_w, xb], axis=0),
                          preferred_element_type=F32).astype(BF16) for xb in xs]
        for x, (_, rhs, rows, cols, u_ref, w_ref) in zip(xs, chains):
            uw = jnp.dot(x, rhs, preferred_element_type=F32)
            u_ref[0, rows, cols] = uw[:, 0:GDN_DV].astype(BF16)
            w_ref[0, rows, cols] = uw[:, GDN_DV:].astype(BF16)

    group = 2
    pending = None
    for g0 in range(0, hps, group):
        loaded = [load_head(hh) for hh in range(g0, min(g0 + group, hps))]
        if pending is not None:
            invert_and_apply(pending)
        pending = []
        for ld in loaded:
            pending += setup_head(ld)
    invert_and_apply(pending)


def _gdn_prep(proj, conv_w, gpar, b, s, hps):
    nt = s // TILE
    hs = GDN_HEADS
    rows8 = TILE // SUBLANES
    last8 = b * s // SUBLANES - 1
    wid = hps * LANES
    ng = hs // hps

    def main(col0):
        return pl.BlockSpec((TILE, wid), lambda bi, i, h: (bi * nt + i, col0 + h))

    def prev(col0):
        return pl.BlockSpec((SUBLANES, wid),
                            lambda bi, i, h: (jnp.maximum((bi * nt + i) * rows8 - 1, 0), col0 + h))

    def nxt(col0):
        return pl.BlockSpec((SUBLANES, wid),
                            lambda bi, i, h: (jnp.minimum((bi * nt + i + 1) * rows8, last8), col0 + h))

    def cw(col0):
        return pl.BlockSpec((CONV_K, wid), lambda bi, i, h: (0, col0 + h))

    qc, kc, vc = 0, ng, 2 * ng
    seq = jax.ShapeDtypeStruct((b, s, GDN_WIDTH), BF16)
    seq_spec = pl.BlockSpec((1, TILE, wid), lambda bi, i, h: (bi, i, h))
    nchunk = TILE // CHUNK
    egl = jax.ShapeDtypeStruct((b * nt, nchunk, 2 * hs, LANES), F32)
    egl_spec = pl.BlockSpec((1, nchunk, 2 * hs, LANES), lambda bi, i, h: (bi * nt + i, 0, 0, 0))
    return pl.pallas_call(
        _gdn_prep_kernel,
        out_shape=(seq,) * 10 + (egl,),
        grid=(b, nt, ng),
        in_specs=[main(qc), prev(qc), nxt(qc), main(kc), prev(kc), nxt(kc),
                  main(vc), prev(vc), nxt(vc), cw(qc), cw(kc), cw(vc),
                  pl.BlockSpec((TILE, LANES), lambda bi, i, h: (bi * nt + i, COL_GATE // LANES)),
                  pl.BlockSpec((SUBLANES, LANES), lambda bi, i, h: (0, 0))],
        out_specs=(seq_spec,) * 10 + (egl_spec,),
        scratch_shapes=[pltpu.VMEM((TILE + 2 * SUBLANES, wid), F32)] * 3 + [
                        pltpu.VMEM((TILE, LANES), F32), pltpu.VMEM((TILE, LANES), F32),
                        pltpu.VMEM((TILE, LANES), F32), pltpu.VMEM((LANES, 2 * TILE), F32),
                        pltpu.VMEM((LANES, 2 * TILE), F32)],
        compiler_params=_cparams("parallel", "parallel", "arbitrary"),
        name="gdn_prep",
    )(*([proj] * 9), conv_w, conv_w, conv_w, proj, gpar)


def _gdn_scan_kernel(uf_ref, wf_ref, qdf_ref, kdf_ref, atf_ref, eglf_ref,
                     ub_ref, wb_ref, qdb_ref, kdb_ref, atb_ref, eglb_ref,
                     of_ref, ob_ref, st_ref):
    @pl.when(pl.program_id(1) == 0)
    def _():
        st_ref[...] = jnp.zeros_like(st_ref)

    nchunk = TILE // CHUNK
    tn_dims = (((0,), (0,)), ((), ()))

    fwd_refs = (uf_ref, wf_ref, qdf_ref, kdf_ref, atf_ref, eglf_ref, of_ref)
    bwd_refs = (ub_ref, wb_ref, qdb_ref, kdb_ref, atb_ref, eglb_ref, ob_ref)

    def step(c, carry):
        chains = []
        for hd in range(GDN_HEADS):
            chains.append((hd, c, hd, fwd_refs))
            chains.append((GDN_HEADS + hd, nchunk - 1 - c, hd, bwd_refs))
        stage1 = []
        for idx, cc, hd, (u_ref, w_ref, qd_ref, kd_ref, at_ref, egl_ref, o_ref) in chains:
            rows = pl.ds(pl.multiple_of(cc * CHUNK, CHUNK), CHUNK)
            cols = slice(hd * LANES, (hd + 1) * LANES)
            sb = st_ref[idx].astype(BF16)
            wq = jnp.concatenate([w_ref[0, rows, cols], qd_ref[0, rows, cols]], axis=0)
            wqs = jnp.dot(wq, sb, preferred_element_type=F32)
            stage1.append((rows, cols, wqs[0:CHUNK], wqs[CHUNK:]))
        for (idx, cc, hd, refs), (rows, cols, ws, qs) in zip(chains, stage1):
            u_ref, w_ref, qd_ref, kd_ref, at_ref, egl_ref, o_ref = refs
            v_new = (u_ref[0, rows, cols].astype(F32) - ws).astype(BF16)
            v_pair = jnp.concatenate([v_new, v_new], axis=0)
            o_ref[0, rows, cols] = (qs + jnp.dot(at_ref[0, rows, cols], v_pair,
                                                 preferred_element_type=F32)).astype(o_ref.dtype)
            decay = egl_ref[0, pl.ds(cc, 1), idx, :]
            upd = lax.dot_general(kd_ref[0, rows, cols], v_new, tn_dims,
                                  preferred_element_type=F32)
            st_ref[idx] = st_ref[idx] * decay + upd
        return carry

    lax.fori_loop(0, nchunk, step, 0)


def _gdn_scan(prep, b, s):
    uf, wf, qdf, kdf, atf, ub, wb, qdb, kdb, atb, egl = prep
    nt = s // TILE
    hs = GDN_HEADS
    nchunk = TILE // CHUNK
    fwd = pl.BlockSpec((1, TILE, GDN_WIDTH), lambda bi, i: (bi, i, 0))
    bwd = pl.BlockSpec((1, TILE, GDN_WIDTH), lambda bi, i: (bi, nt - 1 - i, 0))
    egl_f = pl.BlockSpec((1, nchunk, 2 * hs, LANES), lambda bi, i: (bi * nt + i, 0, 0, 0))
    egl_b = pl.BlockSpec((1, nchunk, 2 * hs, LANES), lambda bi, i: (bi * nt + nt - 1 - i, 0, 0, 0))
    out = jax.ShapeDtypeStruct((b, s, GDN_WIDTH), BF16)
    return pl.pallas_call(
        _gdn_scan_kernel,
        out_shape=(out, out),
        grid=(b, nt),
        in_specs=[fwd] * 5 + [egl_f] + [bwd] * 5 + [egl_b],
        out_specs=(fwd, bwd),
        scratch_shapes=[pltpu.VMEM((2 * hs, GDN_DK, GDN_DV), F32)],
        compiler_params=_cparams("parallel", "arbitrary"),
        name="gdn_scan",
    )(uf, wf, qdf, kdf, atf, egl, ub, wb, qdb, kdb, atb, egl)


def _outproj_kernel(of_ref, ob_ref, za_ref, mixb_ref, x_ref, gn_ref, gp_ref, w_ref, y_ref):
    o = of_ref[...].astype(F32) + ob_ref[...].astype(F32)
    za = za_ref[...]
    parts = []
    for h in range(GDN_HEADS):
        sl = slice(h * GDN_DV, (h + 1) * GDN_DV)
        parts.append((_rms(o[:, sl], gn_ref[...]) * _silu(za[:, sl])).astype(BF16))
    mix_a = jnp.concatenate(parts, axis=1)
    y = jnp.dot(mix_a, w_ref[0:GDN_WIDTH, :], preferred_element_type=F32)
    y = y + jnp.dot(mixb_ref[...], w_ref[GDN_WIDTH:, :], preferred_element_type=F32)
    y_ref[...] = x_ref[...] + _rms(y, gp_ref[...])


def _outproj(o_f, o_b, proj, mix_b, x2d, gn, gp, w_out, tm):
    t = x2d.shape[0]
    row = lambda i: (i, 0)
    const = lambda i: (0, 0)
    return pl.pallas_call(
        _outproj_kernel,
        out_shape=jax.ShapeDtypeStruct((t, D_MODEL), F32),
        grid=(t // tm,),
        in_specs=[
            pl.BlockSpec((tm, GDN_WIDTH), row),
            pl.BlockSpec((tm, GDN_WIDTH), row),
            pl.BlockSpec((tm, GDN_WIDTH), lambda i: (i, COL_ZA // GDN_WIDTH)),
            pl.BlockSpec((tm, MLA_WIDTH), row),
            pl.BlockSpec((tm, D_MODEL), row),
            pl.BlockSpec((1, GDN_DV), const),
            pl.BlockSpec((1, D_MODEL), const),
            pl.BlockSpec((D_MIX, D_MODEL), const),
        ],
        out_specs=pl.BlockSpec((tm, D_MODEL), row),
        compiler_params=_cparams("parallel"),
        name="outproj",
    )(o_f, o_b, proj, mix_b, x2d, gn, gp, w_out)


def _swap_halves(w):
    half = MLA_ROPE // 2
    return jnp.concatenate([w[..., half:], w[..., :half]], axis=-1)


def _prep_layer(w_in, mla_w_uq, mla_w_ukv, w_out, a_log, dt_bias):
    o = np.cumsum((0, GDN_QKV, GDN_WIDTH, 2 * GDN_HEADS, 2 * GDN_HEADS, Q_LORA, KV_LORA,
                   MLA_ROPE, MLA_WIDTH))
    w_in = w_in.astype(BF16)
    qkv, za, bl, al, cq, ckv, kpe, zb = (w_in[:, o[n]:o[n + 1]] for n in range(8))
    pad = jnp.zeros((D_MODEL, LANES - 4 * GDN_HEADS), BF16)
    w_pad = jnp.concatenate([qkv, za, zb, cq, ckv, kpe, _swap_halves(kpe), bl, al, pad], axis=1)
    wq = mla_w_uq.reshape(Q_LORA, MLA_HEADS, MLA_NOPE + MLA_ROPE)
    wq_ext = jnp.concatenate([wq, _swap_halves(wq[..., MLA_NOPE:])], axis=-1)
    wq_ext = wq_ext.reshape(Q_LORA, MLA_HEADS * QK_PAD).astype(BF16)
    wkv = mla_w_ukv.astype(BF16)
    gpar = jnp.zeros((SUBLANES, LANES), F32)
    n_gate = 2 * GDN_HEADS
    gpar = gpar.at[0, n_gate:2 * n_gate].set(a_log.reshape(-1))
    gpar = gpar.at[1, n_gate:2 * n_gate].set(dt_bias.reshape(-1))
    return w_pad, wq_ext, wkv, w_out.astype(BF16), gpar


def _rope_tables(s):
    pos = jnp.arange(s, dtype=F32)
    inv = ROPE_BASE ** (-jnp.arange(0, MLA_ROPE, 2, dtype=F32) / MLA_ROPE)
    ang = pos[:, None] * inv[None, :]
    cos, sin = jnp.cos(ang), jnp.sin(ang)
    zero = jnp.zeros((s, LANES - MLA_ROPE), F32)
    return (jnp.concatenate([cos, cos, zero], axis=1),
            jnp.concatenate([-sin, sin, zero], axis=1))


def _pick(n, prefs):
    for p in prefs:
        if n % p == 0:
            return p
    return n


def _attn_tiles(s):
    tq = _pick(s, (1024, 512, 256, 128))
    resident = 2 * s * ((2 * QK_PAD + V_PAD) * 2 + LANES * 4 + MLA_DV * 2)
    for tk in (2048, 1024, 512, 256, 128):
        scratch = tq * tk * (4 + 4 + 2)
        if (s // 2) % tk == 0 and resident + scratch <= ATTN_VMEM_BUDGET:
            return tq, tk
    raise ValueError(f"no attention tiling for sequence length {s}")


def _tiles(b, s):
    t = b * s
    tq, tk = _attn_tiles(s)
    return dict(tm_in=_pick(t, (1024, 512, 256)), tn_in=2048,
                ts_mla=_pick(s, (1024, 512, 256)), tq=tq, tk=tk,
                tm_out=_pick(t, (512, 256)), gdn_hps=8)


def _layer(x2d, b, s, tl, pre_g, post_g, conv_w, gdn_norm_g, q_norm_g, kv_norm_g,
           w_pad, wq_ext, wkv, w_out, gpar, cos2, sin2):
    proj = _inproj(x2d, pre_g.reshape(1, -1), w_pad, tl["tm_in"], tl["tn_in"])
    prep = _gdn_prep(proj, conv_w, gpar, b, s, tl["gdn_hps"])
    o_f, o_b = _gdn_scan(prep, b, s)
    q, k, v = _mla_proj(proj, q_norm_g.reshape(1, -1), kv_norm_g.reshape(1, -1), wq_ext, wkv,
                        cos2, sin2, b, s, tl["ts_mla"])
    mix_b = _attention(q, k, v, proj, b, s, tl["tq"], tl["tk"])
    return _outproj(o_f.reshape(b * s, -1), o_b.reshape(b * s, -1), proj,
                    mix_b.reshape(b * s, -1), x2d, gdn_norm_g.reshape(1, -1),
                    post_g.reshape(1, -1), w_out, tl["tm_out"])


def _trunk(x, layers, pre_norm_g, post_norm_g, conv_w, gdn_norm_g, mla_q_norm_g, mla_kv_norm_g):
    b, s, d = x.shape
    assert d == D_MODEL and s % TILE == 0
    tl = _tiles(b, s)
    cos2, sin2 = _rope_tables(s)
    x2d = x.reshape(b * s, d)
    for l in range(DEPTH):
        x2d = _layer(x2d, b, s, tl, pre_norm_g[l], post_norm_g[l], conv_w[l], gdn_norm_g[l],
                     mla_q_norm_g[l], mla_kv_norm_g[l], *layers[l], cos2, sin2)
    return x2d.reshape(b, s, d)


def kernel(x_prompt, x_sample, pre_norm_g, post_norm_g, w_in, conv_w, gdn_a_log, gdn_dt_bias,
           gdn_norm_g, mla_q_norm_g, mla_kv_norm_g, mla_w_uq, mla_w_ukv, w_out):
    layers = [_prep_layer(w_in[l], mla_w_uq[l], mla_w_ukv[l], w_out[l], gdn_a_log[l],
                          gdn_dt_bias[l]) for l in range(DEPTH)]
    args = (layers, pre_norm_g, post_norm_g, conv_w, gdn_norm_g, mla_q_norm_g, mla_kv_norm_g)
    return (_trunk(x_prompt, *args), _trunk(x_sample, *args))
```

```python
import functools

import numpy as np
import jax
import jax.numpy as jnp
from jax import lax
from jax.experimental import pallas as pl
from jax.experimental.pallas import tpu as pltpu

F32 = jnp.float32
BF16 = jnp.bfloat16

D_MODEL = 2048
DEPTH = 2
GDN_HEADS = 8
GDN_DK = 128
GDN_DV = 128
GDN_WIDTH = GDN_HEADS * GDN_DV
GDN_QKV = 2 * GDN_HEADS * GDN_DK + GDN_WIDTH
CONV_K = 5
CHUNK = 64
MLA_HEADS = 8
MLA_NOPE = 128
MLA_ROPE = 64
MLA_DV = 128
MLA_WIDTH = MLA_HEADS * MLA_DV
Q_LORA = 512
KV_LORA = 256
ROPE_BASE = 10000.0
D_MIX = GDN_WIDTH + MLA_WIDTH
EPS = 1e-6
MLA_SCALE = (MLA_NOPE + MLA_ROPE) ** -0.5
GDN_SCALE = GDN_DK ** -0.5
LOG2_E = float(np.log2(np.e))

LANES = 128
SUBLANES = 8
VMEM_LIMIT = 56 * 1024 * 1024
ATTN_VMEM_BUDGET = 50 * 1024 * 1024

COL_QKV = 0
COL_ZA = COL_QKV + GDN_QKV
COL_ZB = COL_ZA + GDN_WIDTH
COL_CQ = COL_ZB + MLA_WIDTH
COL_CKV = COL_CQ + Q_LORA
COL_KPE = COL_CKV + KV_LORA
COL_GATE = COL_KPE + 2 * MLA_ROPE
D_PROJ = COL_GATE + LANES
MISC_W = D_PROJ - COL_CQ
QK_PAD = 2 * LANES
V_PAD = 2 * LANES
QK_SCALE_LOG2 = MLA_SCALE * LOG2_E
TILE = 4 * CHUNK
SCAN_TILES = 2
CHUNK_LOG2 = CHUNK.bit_length() - 1
assert 1 << CHUNK_LOG2 == CHUNK and 2 * CHUNK == LANES

assert D_PROJ == 6144 and MISC_W == 1024 and COL_GATE - COL_CQ == 896


def _cparams(*sem):
    return pltpu.CompilerParams(dimension_semantics=sem, vmem_limit_bytes=VMEM_LIMIT)


def _silu(z):
    return z / (1.0 + jnp.exp(-z))


def _rms(x, g):
    return x * lax.rsqrt(jnp.mean(x * x, axis=-1, keepdims=True) + EPS) * g


def _inproj_kernel(x_ref, g_ref, w_ref, o_ref, h_ref):
    @pl.when(pl.program_id(1) == 0)
    def _():
        h_ref[...] = _rms(x_ref[...], g_ref[...]).astype(BF16)

    o_ref[...] = jnp.dot(h_ref[...], w_ref[...], preferred_element_type=F32)


def _inproj(x2d, g, w_pad, tm, tn):
    t = x2d.shape[0]
    return pl.pallas_call(
        _inproj_kernel,
        out_shape=jax.ShapeDtypeStruct((t, D_PROJ), F32),
        grid=(t // tm, D_PROJ // tn),
        in_specs=[
            pl.BlockSpec((tm, D_MODEL), lambda i, j: (i, 0)),
            pl.BlockSpec((1, D_MODEL), lambda i, j: (0, 0)),
            pl.BlockSpec((D_MODEL, tn), lambda i, j: (0, j)),
        ],
        out_specs=pl.BlockSpec((tm, tn), lambda i, j: (i, j)),
        scratch_shapes=[pltpu.VMEM((tm, D_MODEL), BF16)],
        compiler_params=_cparams("parallel", "arbitrary"),
        name="inproj",
    )(x2d, g, w_pad)


def _mla_proj_kernel(p_ref, gq_ref, gkv_ref, wq_ref, wkv_ref, cos_ref, sin_ref,
                     q_ref, k_ref, v_ref):
    t = p_ref[...]
    cqn = _rms(t[:, 0:Q_LORA], gq_ref[...]).astype(BF16)
    ckn = _rms(t[:, Q_LORA:Q_LORA + KV_LORA], gkv_ref[...]).astype(BF16)
    kpe2 = t[:, COL_KPE - COL_CQ:COL_GATE - COL_CQ]
    qe = jnp.dot(cqn, wq_ref[...], preferred_element_type=F32)
    kve = jnp.dot(ckn, wkv_ref[...], preferred_element_type=F32)
    c2 = cos_ref[...]
    s2 = sin_ref[...]
    krope = (kpe2 * c2 + pltpu.roll(kpe2, MLA_ROPE, axis=1) * s2).astype(BF16)
    ones = jnp.ones((t.shape[0], V_PAD - MLA_DV), BF16)
    for h in range(MLA_HEADS):
        qn = qe[:, h * QK_PAD:h * QK_PAD + LANES]
        qp = qe[:, h * QK_PAD + LANES:(h + 1) * QK_PAD]
        qr = qp * c2 + pltpu.roll(qp, MLA_ROPE, axis=1) * s2
        q_ref[0, h, :, 0:LANES] = (qn * QK_SCALE_LOG2).astype(BF16)
        q_ref[0, h, :, LANES:QK_PAD] = (qr * QK_SCALE_LOG2).astype(BF16)
        k_ref[0, h, :, 0:LANES] = kve[:, h * QK_PAD:h * QK_PAD + LANES].astype(BF16)
        k_ref[0, h, :, LANES:QK_PAD] = krope
        v_ref[0, h, :, 0:MLA_DV] = kve[:, h * QK_PAD + LANES:(h + 1) * QK_PAD].astype(BF16)
        v_ref[0, h, :, MLA_DV:] = ones


def _mla_proj(proj, gq, gkv, wq_ext, wkv, cos2, sin2, b, s, ts):
    nt = s // ts
    hs = MLA_HEADS
    return pl.pallas_call(
        _mla_proj_kernel,
        out_shape=(jax.ShapeDtypeStruct((b, hs, s, QK_PAD), BF16),
                   jax.ShapeDtypeStruct((b, hs, s, QK_PAD), BF16),
                   jax.ShapeDtypeStruct((b, hs, s, V_PAD), BF16)),
        grid=(b, nt),
        in_specs=[
            pl.BlockSpec((ts, MISC_W), lambda bi, i: (bi * nt + i, COL_CQ // MISC_W)),
            pl.BlockSpec((1, Q_LORA), lambda bi, i: (0, 0)),
            pl.BlockSpec((1, KV_LORA), lambda bi, i: (0, 0)),
            pl.BlockSpec((Q_LORA, hs * QK_PAD), lambda bi, i: (0, 0)),
            pl.BlockSpec((KV_LORA, hs * QK_PAD), lambda bi, i: (0, 0)),
            pl.BlockSpec((ts, LANES), lambda bi, i: (i, 0)),
            pl.BlockSpec((ts, LANES), lambda bi, i: (i, 0)),
        ],
        out_specs=(pl.BlockSpec((1, hs, ts, QK_PAD), lambda bi, i: (bi, 0, i, 0)),
                   pl.BlockSpec((1, hs, ts, QK_PAD), lambda bi, i: (bi, 0, i, 0)),
                   pl.BlockSpec((1, hs, ts, V_PAD), lambda bi, i: (bi, 0, i, 0))),
        compiler_params=_cparams("parallel", "parallel"),
        name="mla_proj",
    )(proj, gq, gkv, wq_ext, wkv, cos2, sin2)


def _attn_kernel(q_ref, k_ref, v_ref, z_ref, o_ref, sa_ref, sb_ref, p_ref, m_ref, acc_ref,
                 *, tq, tk, rb):
    s_len = k_ref.shape[2]
    nq, nk = s_len // tq, s_len // tk
    nt_dims = (((1,), (1,)), ((), ()))
    blocks = [slice(r * rb, (r + 1) * rb) for r in range(tq // rb)]
    lane_tiles = [slice(t * LANES, (t + 1) * LANES) for t in range(tk // LANES)]

    def scores(i, j, s_ref):
        q = q_ref[0, 0, pl.ds(pl.multiple_of(i * tq, tq), tq), :]
        s_ref[...] = lax.dot_general(q, k_ref[0, 0, j * tk:(j + 1) * tk, :], nt_dims,
                                     preferred_element_type=F32)

    def softmax_and_pv(j, s_ref):
        mx_parts = []
        for rows in blocks:
            mx = s_ref[rows, lane_tiles[0]]
            for lt in lane_tiles[1:]:
                mx = jnp.maximum(mx, s_ref[rows, lt])
            mx_parts.append(mx)
        mx_all = jnp.concatenate(mx_parts, axis=0)
        row_max = jnp.broadcast_to(jnp.max(mx_all, axis=-1, keepdims=True), mx_all.shape)
        if j == 0:
            m_new = row_max
        else:
            m_old = m_ref[...]
            m_new = jnp.maximum(m_old, row_max)
            alpha = jnp.exp2(m_old - m_new)
        m_ref[...] = m_new
        for rows in blocks:
            m_b = m_new[rows]
            for lt in lane_tiles:
                p_ref[rows, lt] = jnp.exp2(s_ref[rows, lt] - m_b).astype(BF16)
        pv = jnp.dot(p_ref[...], v_ref[0, 0, j * tk:(j + 1) * tk, :],
                     preferred_element_type=F32)
        if j == 0:
            acc_ref[...] = pv
        else:
            acc_ref[...] = acc_ref[...] * jnp.concatenate([alpha, alpha], axis=1) + pv

    bufs = (sa_ref, sb_ref)
    scores(0, 0, sa_ref)

    def body(i, carry):
        for j in range(nk):
            if j + 1 < nk:
                scores(i, j + 1, bufs[(j + 1) % 2])
            else:
                scores(jnp.minimum(i + 1, nq - 1), 0, bufs[0])
            softmax_and_pv(j, bufs[j % 2])
        rows = pl.ds(pl.multiple_of(i * tq, tq), tq)
        acc = acc_ref[...]
        o_ref[0, rows, :] = ((acc[:, 0:MLA_DV] / acc[:, MLA_DV:])
                             * _silu(z_ref[rows, :])).astype(BF16)
        return carry

    lax.fori_loop(0, nq, body, 0)


def _attention(q, k, v, proj, b, s, tq, tk):
    hs = MLA_HEADS
    zb0 = COL_ZB // LANES
    assert (s // tk) % 2 == 0
    return pl.pallas_call(
        functools.partial(_attn_kernel, tq=tq, tk=tk, rb=2 * SUBLANES),
        out_shape=jax.ShapeDtypeStruct((b, s, MLA_WIDTH), BF16),
        grid=(b, hs),
        in_specs=[
            pl.BlockSpec((1, 1, s, QK_PAD), lambda bi, h: (bi, h, 0, 0)),
            pl.BlockSpec((1, 1, s, QK_PAD), lambda bi, h: (bi, h, 0, 0)),
            pl.BlockSpec((1, 1, s, V_PAD), lambda bi, h: (bi, h, 0, 0)),
            pl.BlockSpec((s, LANES), lambda bi, h: (bi, zb0 + h)),
        ],
        out_specs=pl.BlockSpec((1, s, MLA_DV), lambda bi, h: (bi, 0, h)),
        scratch_shapes=[pltpu.VMEM((tq, tk), F32), pltpu.VMEM((tq, tk), F32),
                        pltpu.VMEM((tq, tk), BF16),
                        pltpu.VMEM((tq, LANES), F32), pltpu.VMEM((tq, V_PAD), F32)],
        compiler_params=_cparams("parallel", "arbitrary"),
        name="attention",
    )(q, k, v, proj)


def _tile_masks():
    r = lax.broadcasted_iota(jnp.int32, (TILE, TILE), 0)
    c = lax.broadcasted_iota(jnp.int32, (TILE, TILE), 1)
    same = (r >> CHUNK_LOG2) == (c >> CHUNK_LOG2)
    return (same & (r >= c)).astype(F32), (same & (r <= c)).astype(F32)


def _col(x, lane):
    li = lax.broadcasted_iota(jnp.int32, x.shape, 1)
    col = jnp.sum(jnp.where(li == lane, x, 0.0), axis=1, keepdims=True)
    return jnp.broadcast_to(col, x.shape)


def _dup_chunks_t(cs):
    parts = []
    for c in range(TILE // CHUNK):
        blk = cs[c * CHUNK:(c + 1) * CHUNK, :]
        parts += [blk, blk]
    return jnp.concatenate(parts, axis=0).T


def _gdn_prep_kernel(q_ref, qp_ref, qn_ref, k_ref, kp_ref, kn_ref, v_ref, vp_ref, vn_ref,
                     cwq_ref, cwk_ref, cwv_ref, gate_ref, gpar_ref,
                     uf_ref, wf_ref, qdf_ref, kdf_ref, atf_ref,
                     ub_ref, wb_ref, qdb_ref, kdb_ref, atb_ref, egl_ref,
                     xq_ref, xk_ref, xv_ref, g_ref, csl_ref, csu_ref, cslt_ref, csut_ref):
    i = pl.program_id(1)
    nt = pl.num_programs(1)
    hg = pl.program_id(2)
    hps = q_ref.shape[1] // LANES

    @pl.when(hg == 0)
    def _():
        t = gate_ref[...]
        a_log = gpar_ref[0:1, :]
        dt_b = gpar_ref[1:2, :]
        li = lax.broadcasted_iota(jnp.int32, t.shape, 1)
        xg = t + dt_b
        sp = jnp.maximum(xg, 0.0) + jnp.log1p(jnp.exp(-jnp.abs(xg)))
        gdec = -jnp.exp(a_log) * sp
        beta = 1.0 / (1.0 + jnp.exp(-t))
        n_gate = 2 * GDN_HEADS
        g = jnp.where(li < n_gate, beta, jnp.where(li < 2 * n_gate, gdec, 0.0))
        g_ref[...] = g
        tri_l, tri_u = _tile_masks()
        hi = g.astype(BF16)
        r1 = g - hi.astype(F32)
        mid = r1.astype(BF16)
        lo = (r1 - mid.astype(F32)).astype(BF16)
        pieces = jnp.concatenate([hi, mid, lo], axis=1)

        def cumsum(tri):
            c3 = jnp.dot(tri.astype(BF16), pieces, preferred_element_type=F32)
            return c3[:, 0:LANES] + c3[:, LANES:2 * LANES] + c3[:, 2 * LANES:]

        csl = cumsum(tri_l)
        csu = cumsum(tri_u)
        csl_ref[...] = csl
        csu_ref[...] = csu
        cslt_ref[...] = _dup_chunks_t(csl)
        csut_ref[...] = _dup_chunks_t(csu)

    def fill_halo(xs_ref, main_ref, prev_ref, next_ref):
        xs_ref[0:SUBLANES, :] = jnp.where(i > 0, prev_ref[...], 0.0)
        xs_ref[SUBLANES:SUBLANES + TILE, :] = main_ref[...]
        xs_ref[SUBLANES + TILE:, :] = jnp.where(i < nt - 1, next_ref[...], 0.0)

    fill_halo(xq_ref, q_ref, qp_ref, qn_ref)
    fill_halo(xk_ref, k_ref, kp_ref, kn_ref)
    fill_halo(xv_ref, v_ref, vp_ref, vn_ref)

    def conv_silu(xs_ref, w_ref, cols):
        pad = (CONV_K - 1) // 2
        xs = xs_ref[:, cols]
        n = xs.shape[0]
        acc = None
        for tap in range(CONV_K):
            sh = xs if tap == pad else pltpu.roll(xs, (pad - tap) % n, axis=0)
            term = sh[SUBLANES:SUBLANES + TILE, :] * w_ref[tap:tap + 1, cols]
            acc = term if acc is None else acc + term
        return _silu(acc)

    def l2n(x):
        return x * lax.rsqrt(jnp.sum(x * x, axis=-1, keepdims=True) + EPS)

    ri = lax.broadcasted_iota(jnp.int32, (CHUNK, LANES), 0)
    li = lax.broadcasted_iota(jnp.int32, (CHUNK, LANES), 1)
    lj = li & (CHUNK - 1)
    left = li < CHUNK
    left_f = left.astype(F32)
    eye_left = (ri == li).astype(F32)
    zeros_r = jnp.zeros((CHUNK, 2 * LANES), BF16)
    nt_dims = (((1,), (1,)), ((), ()))
    nchunk = TILE // CHUNK

    g = g_ref[...]
    csl = csl_ref[...]
    csu = csu_ref[...]
    nh = GDN_HEADS
    lower = (ri >= lj).astype(F32)
    slower = (ri > lj).astype(F32)
    upper = (ri <= lj).astype(F32)
    supper = (ri < lj).astype(F32)

    def load_head(hh):
        cols = slice(hh * LANES, (hh + 1) * LANES)
        q = l2n(conv_silu(xq_ref, cwq_ref, cols)) * GDN_SCALE
        k = l2n(conv_silu(xk_ref, cwk_ref, cols))
        v = conv_silu(xv_ref, cwv_ref, cols)
        return hh, q, k, v

    def setup_head(loaded):
        hh, q, k, v = loaded
        h = hg * hps + hh
        cols = slice(hh * LANES, (hh + 1) * LANES)
        qb = q.astype(BF16)
        kb16 = k.astype(BF16)
        grams = []
        for c in range(nchunk):
            rows = slice(c * CHUNK, (c + 1) * CHUNK)
            k2 = jnp.concatenate([kb16[rows], kb16[rows]], axis=0)
            grams.append((lax.dot_general(kb16[rows], k2, nt_dims, preferred_element_type=F32),
                          lax.dot_general(qb[rows], k2, nt_dims, preferred_element_type=F32)))

        def setup_direction(beta_lane, g_lane, cs, cs_other, cst_ref, tri, stri,
                            u_ref, w_ref, qd_ref, kd_ref, at_ref, egl_row):
            beta = _col(g, beta_lane)
            gc = _col(cs, g_lane)
            rest = _col(cs_other - g, g_lane)
            egc = jnp.exp(gc)
            qd_ref[0, :, cols] = (q * egc).astype(BF16)
            kd_ref[0, :, cols] = (k * jnp.exp(rest)).astype(BF16)
            vb = (v * beta).astype(BF16)
            kbg = (k * (beta * egc)).astype(BF16)
            tot = jnp.exp(gc + rest)
            gc_rows = cst_ref[pl.ds(g_lane, 1), :]
            chains = []
            for c in range(nchunk):
                rows = slice(c * CHUNK, (c + 1) * CHUNK)
                kk2, qk2 = grams[c]
                gc_row = gc_rows[:, c * LANES:(c + 1) * LANES]
                decay = jnp.exp(jnp.minimum(gc[rows] - gc_row, 0.0))
                x = jnp.where(left, eye_left, -(beta[rows] * kk2 * decay * stri))
                half = left_f if c % 2 == 0 else 1.0 - left_f
                at_ref[0, rows, cols] = (qk2 * decay * (tri * half)).astype(BF16)
                egl_ref[0, c, pl.ds(egl_row, 1), :] = tot[c * CHUNK:c * CHUNK + 1, :]
                rhs = jnp.concatenate(
                    [jnp.concatenate([vb[rows], kbg[rows]], axis=1), zeros_r], axis=0)
                chains.append((x, rhs, rows, cols, u_ref, w_ref))
            return chains

        return (setup_direction(h, 2 * nh + h, csl, csu, cslt_ref, lower, slower,
                                uf_ref, wf_ref, qdf_ref, kdf_ref, atf_ref, h)
                + setup_direction(nh + h, 3 * nh + h, csu, csl, csut_ref, upper, supper,
                                  ub_ref, wb_ref, qdb_ref, kdb_ref, atb_ref, nh + h))

    eye_w = eye_left.astype(BF16)

    def invert_and_apply(chains):
        xs = [ch[0].astype(BF16) for ch in chains]
        for _ in range(CHUNK_LOG2):
            xs = [jnp.dot(xb, jnp.concatenate([eye_w, xb], axis=0),
                          preferred_element_type=F32).astype(BF16) for xb in xs]
        for x, (_, rhs, rows, cols, u_ref, w_ref) in zip(xs, chains):
            uw = jnp.dot(x, rhs, preferred_element_type=F32)
            u_ref[0, rows, cols] = uw[:, 0:GDN_DV].astype(BF16)
            w_ref[0, rows, cols] = uw[:, GDN_DV:].astype(BF16)

    group = 2
    pending = None
    for g0 in range(0, hps, group):
        loaded = [load_head(hh) for hh in range(g0, min(g0 + group, hps))]
        if pending is not None:
            invert_and_apply(pending)
        pending = []
        for ld in loaded:
            pending += setup_head(ld)
    invert_and_apply(pending)


def _gdn_prep(proj, conv_w, gpar, b, s, hps):
    nt = s // TILE
    hs = GDN_HEADS
    rows8 = TILE // SUBLANES
    last8 = b * s // SUBLANES - 1
    wid = hps * LANES
    ng = hs // hps

    def main(col0):
        return pl.BlockSpec((TILE, wid), lambda bi, i, h: (bi * nt + i, col0 + h))

    def prev(col0):
        return pl.BlockSpec((SUBLANES, wid),
                            lambda bi, i, h: (jnp.maximum((bi * nt + i) * rows8 - 1, 0), col0 + h))

    def nxt(col0):
        return pl.BlockSpec((SUBLANES, wid),
                            lambda bi, i, h: (jnp.minimum((bi * nt + i + 1) * rows8, last8), col0 + h))

    def cw(col0):
        return pl.BlockSpec((CONV_K, wid), lambda bi, i, h: (0, col0 + h))

    qc, kc, vc = 0, ng, 2 * ng
    seq = jax.ShapeDtypeStruct((b, s, GDN_WIDTH), BF16)
    seq_spec = pl.BlockSpec((1, TILE, wid), lambda bi, i, h: (bi, i, h))
    nchunk = TILE // CHUNK
    egl = jax.ShapeDtypeStruct((b * nt, nchunk, 2 * hs, LANES), F32)
    egl_spec = pl.BlockSpec((1, nchunk, 2 * hs, LANES), lambda bi, i, h: (bi * nt + i, 0, 0, 0))
    return pl.pallas_call(
        _gdn_prep_kernel,
        out_shape=(seq,) * 10 + (egl,),
        grid=(b, nt, ng),
        in_specs=[main(qc), prev(qc), nxt(qc), main(kc), prev(kc), nxt(kc),
                  main(vc), prev(vc), nxt(vc), cw(qc), cw(kc), cw(vc),
                  pl.BlockSpec((TILE, LANES), lambda bi, i, h: (bi * nt + i, COL_GATE // LANES)),
                  pl.BlockSpec((SUBLANES, LANES), lambda bi, i, h: (0, 0))],
        out_specs=(seq_spec,) * 10 + (egl_spec,),
        scratch_shapes=[pltpu.VMEM((TILE + 2 * SUBLANES, wid), F32)] * 3 + [
                        pltpu.VMEM((TILE, LANES), F32), pltpu.VMEM((TILE, LANES), F32),
                        pltpu.VMEM((TILE, LANES), F32), pltpu.VMEM((LANES, 2 * TILE), F32),
                        pltpu.VMEM((LANES, 2 * TILE), F32)],
        compiler_params=_cparams("parallel", "parallel", "arbitrary"),
        name="gdn_prep",
    )(*([proj] * 9), conv_w, conv_w, conv_w, proj, gpar)


def _gdn_scan_kernel(uf_ref, wf_ref, qdf_ref, kdf_ref, atf_ref, eglf_ref,
                     ub_ref, wb_ref, qdb_ref, kdb_ref, atb_ref, eglb_ref,
                     of_ref, ob_ref, st_ref):
    @pl.when(pl.program_id(1) == 0)
    def _():
        st_ref[...] = jnp.zeros_like(st_ref)

    nchunk = uf_ref.shape[1] // CHUNK
    per_tile = eglf_ref.shape[1]
    tn_dims = (((0,), (0,)), ((), ()))

    fwd_refs = (uf_ref, wf_ref, qdf_ref, kdf_ref, atf_ref, eglf_ref, of_ref)
    bwd_refs = (ub_ref, wb_ref, qdb_ref, kdb_ref, atb_ref, eglb_ref, ob_ref)

    def step(c, carry):
        chains = []
        for hd in range(GDN_HEADS):
            chains.append((hd, c, hd, fwd_refs))
            chains.append((GDN_HEADS + hd, nchunk - 1 - c, hd, bwd_refs))
        stage1 = []
        for idx, cc, hd, (u_ref, w_ref, qd_ref, kd_ref, at_ref, egl_ref, o_ref) in chains:
            rows = pl.ds(pl.multiple_of(cc * CHUNK, CHUNK), CHUNK)
            cols = slice(hd * LANES, (hd + 1) * LANES)
            sb = st_ref[idx].astype(BF16)
            wq = jnp.concatenate([w_ref[0, rows, cols], qd_ref[0, rows, cols]], axis=0)
            wqs = jnp.dot(wq, sb, preferred_element_type=F32)
            stage1.append((rows, cols, wqs[0:CHUNK], wqs[CHUNK:]))
        for (idx, cc, hd, refs), (rows, cols, ws, qs) in zip(chains, stage1):
            u_ref, w_ref, qd_ref, kd_ref, at_ref, egl_ref, o_ref = refs
            v_new = (u_ref[0, rows, cols].astype(F32) - ws).astype(BF16)
            v_pair = jnp.concatenate([v_new, v_new], axis=0)
            o_ref[0, rows, cols] = (qs + jnp.dot(at_ref[0, rows, cols], v_pair,
                                                 preferred_element_type=F32)).astype(o_ref.dtype)
            decay = egl_ref[cc // per_tile, pl.ds(cc % per_tile, 1), idx, :]
            upd = lax.dot_general(kd_ref[0, rows, cols], v_new, tn_dims,
                                  preferred_element_type=F32)
            st_ref[idx] = st_ref[idx] * decay + upd
        return carry

    lax.fori_loop(0, nchunk, step, 0)


def _gdn_scan(prep, b, s):
    uf, wf, qdf, kdf, atf, ub, wb, qdb, kdb, atb, egl = prep
    span = SCAN_TILES * TILE
    assert s % span == 0
    nt = s // span
    hs = GDN_HEADS
    nchunk = TILE // CHUNK
    fwd = pl.BlockSpec((1, span, GDN_WIDTH), lambda bi, i: (bi, i, 0))
    bwd = pl.BlockSpec((1, span, GDN_WIDTH), lambda bi, i: (bi, nt - 1 - i, 0))
    egl_blk = (SCAN_TILES, nchunk, 2 * hs, LANES)
    egl_f = pl.BlockSpec(egl_blk, lambda bi, i: (bi * nt + i, 0, 0, 0))
    egl_b = pl.BlockSpec(egl_blk, lambda bi, i: (bi * nt + nt - 1 - i, 0, 0, 0))
    out = jax.ShapeDtypeStruct((b, s, GDN_WIDTH), BF16)
    return pl.pallas_call(
        _gdn_scan_kernel,
        out_shape=(out, out),
        grid=(b, nt),
        in_specs=[fwd] * 5 + [egl_f] + [bwd] * 5 + [egl_b],
        out_specs=(fwd, bwd),
        scratch_shapes=[pltpu.VMEM((2 * hs, GDN_DK, GDN_DV), F32)],
        compiler_params=_cparams("parallel", "arbitrary"),
        name="gdn_scan",
    )(uf, wf, qdf, kdf, atf, egl, ub, wb, qdb, kdb, atb, egl)


def _outproj_kernel(of_ref, ob_ref, za_ref, mixb_ref, x_ref, gn_ref, gp_ref, w_ref, y_ref):
    o = of_ref[...].astype(F32) + ob_ref[...].astype(F32)
    za = za_ref[...]
    parts = []
    for h in range(GDN_HEADS):
        sl = slice(h * GDN_DV, (h + 1) * GDN_DV)
        parts.append((_rms(o[:, sl], gn_ref[...]) * _silu(za[:, sl])).astype(BF16))
    mix_a = jnp.concatenate(parts, axis=1)
    y = jnp.dot(mix_a, w_ref[0:GDN_WIDTH, :], preferred_element_type=F32)
    y = y + jnp.dot(mixb_ref[...], w_ref[GDN_WIDTH:, :], preferred_element_type=F32)
    y_ref[...] = x_ref[...] + _rms(y, gp_ref[...])


def _outproj(o_f, o_b, proj, mix_b, x2d, gn, gp, w_out, tm):
    t = x2d.shape[0]
    row = lambda i: (i, 0)
    const = lambda i: (0, 0)
    return pl.pallas_call(
        _outproj_kernel,
        out_shape=jax.ShapeDtypeStruct((t, D_MODEL), F32),
        grid=(t // tm,),
        in_specs=[
            pl.BlockSpec((tm, GDN_WIDTH), row),
            pl.BlockSpec((tm, GDN_WIDTH), row),
            pl.BlockSpec((tm, GDN_WIDTH), lambda i: (i, COL_ZA // GDN_WIDTH)),
            pl.BlockSpec((tm, MLA_WIDTH), row),
            pl.BlockSpec((tm, D_MODEL), row),
            pl.BlockSpec((1, GDN_DV), const),
            pl.BlockSpec((1, D_MODEL), const),
            pl.BlockSpec((D_MIX, D_MODEL), const),
        ],
        out_specs=pl.BlockSpec((tm, D_MODEL), row),
        compiler_params=_cparams("parallel"),
        name="outproj",
    )(o_f, o_b, proj, mix_b, x2d, gn, gp, w_out)


def _swap_halves(w):
    half = MLA_ROPE // 2
    return jnp.concatenate([w[..., half:], w[..., :half]], axis=-1)


def _prep_layer(w_in, mla_w_uq, mla_w_ukv, w_out, a_log, dt_bias):
    o = np.cumsum((0, GDN_QKV, GDN_WIDTH, 2 * GDN_HEADS, 2 * GDN_HEADS, Q_LORA, KV_LORA,
                   MLA_ROPE, MLA_WIDTH))
    w_in = w_in.astype(BF16)
    qkv, za, bl, al, cq, ckv, kpe, zb = (w_in[:, o[n]:o[n + 1]] for n in range(8))
    pad = jnp.zeros((D_MODEL, LANES - 4 * GDN_HEADS), BF16)
    w_pad = jnp.concatenate([qkv, za, zb, cq, ckv, kpe, _swap_halves(kpe), bl, al, pad], axis=1)
    wq = mla_w_uq.reshape(Q_LORA, MLA_HEADS, MLA_NOPE + MLA_ROPE)
    wq_ext = jnp.concatenate([wq, _swap_halves(wq[..., MLA_NOPE:])], axis=-1)
    wq_ext = wq_ext.reshape(Q_LORA, MLA_HEADS * QK_PAD).astype(BF16)
    wkv = mla_w_ukv.astype(BF16)
    gpar = jnp.zeros((SUBLANES, LANES), F32)
    n_gate = 2 * GDN_HEADS
    gpar = gpar.at[0, n_gate:2 * n_gate].set(a_log.reshape(-1))
    gpar = gpar.at[1, n_gate:2 * n_gate].set(dt_bias.reshape(-1))
    return w_pad, wq_ext, wkv, w_out.astype(BF16), gpar


def _rope_tables(s):
    pos = jnp.arange(s, dtype=F32)
    inv = ROPE_BASE ** (-jnp.arange(0, MLA_ROPE, 2, dtype=F32) / MLA_ROPE)
    ang = pos[:, None] * inv[None, :]
    cos, sin = jnp.cos(ang), jnp.sin(ang)
    zero = jnp.zeros((s, LANES - MLA_ROPE), F32)
    return (jnp.concatenate([cos, cos, zero], axis=1),
            jnp.concatenate([-sin, sin, zero], axis=1))


def _pick(n, prefs):
    for p in prefs:
        if n % p == 0:
            return p
    return n


def _attn_tiles(s):
    tq = _pick(s, (1024, 512, 256, 128))
    resident = 2 * s * ((2 * QK_PAD + V_PAD) * 2 + LANES * 4 + MLA_DV * 2)
    for tk in (2048, 1024, 512, 256, 128):
        scratch = tq * tk * (4 + 4 + 2)
        if (s // 2) % tk == 0 and resident + scratch <= ATTN_VMEM_BUDGET:
            return tq, tk
    raise ValueError(f"no attention tiling for sequence length {s}")


def _tiles(b, s):
    t = b * s
    tq, tk = _attn_tiles(s)
    return dict(tm_in=_pick(t, (1024, 512, 256)), tn_in=2048,
                ts_mla=_pick(s, (1024, 512, 256)), tq=tq, tk=tk,
                tm_out=_pick(t, (512, 256)), gdn_hps=8)


def _layer(x2d, b, s, tl, pre_g, post_g, conv_w, gdn_norm_g, q_norm_g, kv_norm_g,
           w_pad, wq_ext, wkv, w_out, gpar, cos2, sin2):
    proj = _inproj(x2d, pre_g.reshape(1, -1), w_pad, tl["tm_in"], tl["tn_in"])
    prep = _gdn_prep(proj, conv_w, gpar, b, s, tl["gdn_hps"])
    o_f, o_b = _gdn_scan(prep, b, s)
    q, k, v = _mla_proj(proj, q_norm_g.reshape(1, -1), kv_norm_g.reshape(1, -1), wq_ext, wkv,
                        cos2, sin2, b, s, tl["ts_mla"])
    mix_b = _attention(q, k, v, proj, b, s, tl["tq"], tl["tk"])
    return _outproj(o_f.reshape(b * s, -1), o_b.reshape(b * s, -1), proj,
                    mix_b.reshape(b * s, -1), x2d, gdn_norm_g.reshape(1, -1),
                    post_g.reshape(1, -1), w_out, tl["tm_out"])


def _trunk(x, layers, pre_norm_g, post_norm_g, conv_w, gdn_norm_g, mla_q_norm_g, mla_kv_norm_g):
    b, s, d = x.shape
    assert d == D_MODEL and s % TILE == 0
    tl = _tiles(b, s)
    cos2, sin2 = _rope_tables(s)
    x2d = x.reshape(b * s, d)
    for l in range(DEPTH):
        x2d = _layer(x2d, b, s, tl, pre_norm_g[l], post_norm_g[l], conv_w[l], gdn_norm_g[l],
                     mla_q_norm_g[l], mla_kv_norm_g[l], *layers[l], cos2, sin2)
    return x2d.reshape(b, s, d)


def kernel(x_prompt, x_sample, pre_norm_g, post_norm_g, w_in, conv_w, gdn_a_log, gdn_dt_bias,
           gdn_norm_g, mla_q_norm_g, mla_kv_norm_g, mla_w_uq, mla_w_ukv, w_out):
    layers = [_prep_layer(w_in[l], mla_w_uq[l], mla_w_ukv[l], w_out[l], gdn_a_log[l],
                          gdn_dt_bias[l]) for l in range(DEPTH)]
    args = (layers, pre_norm_g, post_norm_g, conv_w, gdn_norm_g, mla_q_norm_g, mla_kv_norm_g)
    return (_trunk(x_prompt, *args), _trunk(x_sample, *args))
```
